```python
import math
import jax, jax.numpy as jnp
from jax import lax
import numpy as np

D_MODEL = 1024
BATCH = 16
SEQ = 4096
DEPTH = 1
DEC_BATCH = 128
DEC_SEQ = 1
PAST_LEN = 8192
PAGE_SIZE = 128

MIX_WIDTH = D_MODEL
ATTN_WIDTH = MIX_WIDTH // 2
POOL_WIDTH = MIX_WIDTH - ATTN_WIDTH
DIFF_HEAD_DIM = 64
N_DIFF_HEADS = ATTN_WIDTH // (2 * DIFF_HEAD_DIM)
V_HEAD_DIM = 2 * DIFF_HEAD_DIM
ROT_DIM = DIFF_HEAD_DIM // 4
ROPE_THETA = 500000.0
POOL_WINDOWS = (2, 4, 8, 16)
N_POOL_GROUPS = len(POOL_WINDOWS)
POOL_GROUP_WIDTH = POOL_WIDTH // N_POOL_GROUPS
POOL_CTX = max(POOL_WINDOWS) - 1
IN_WIDTH = 3 * ATTN_WIDTH + POOL_WIDTH
N_EXPERT_GROUPS = 4
EXPERTS_PER_GROUP = 4
EXPERT_FF = 256
FINE_TOP_K = 2
PLE_DIM = 256
Q_BLOCK = 128
RMS_EPS = 1e-6

kernel_name = 'hymba_diffattn_pool_hmoe_step'

F32 = jnp.float32


def rmsnorm(x, g):
    xf = x.astype(F32)
    y = xf * lax.rsqrt(jnp.mean(xf * xf, axis=-1, keepdims=True) + RMS_EPS)
    return (y * g.astype(F32)).astype(x.dtype)


def rope_partial(x, pos):
    half = ROT_DIM // 2
    inv = ROPE_THETA ** (-jnp.arange(0, ROT_DIM, 2, dtype=F32) / ROT_DIM)
    ang = pos.astype(F32)[:, None] * inv[None, :]
    cos = jnp.cos(ang)[None, :, None, None, :]
    sin = jnp.sin(ang)[None, :, None, None, :]
    xf = x.astype(F32)
    x1 = xf[..., :half]
    x2 = xf[..., half:ROT_DIM]
    out = jnp.concatenate([x1 * cos - x2 * sin, x2 * cos + x1 * sin, xf[..., ROT_DIM:]], axis=-1)
    return out.astype(x.dtype)


def mix_inputs(n, w_in, pos):
    B, T, _ = n.shape
    z = n @ w_in
    q = z[..., :ATTN_WIDTH].reshape(B, T, N_DIFF_HEADS, 2, DIFF_HEAD_DIM)
    k = z[..., ATTN_WIDTH:2 * ATTN_WIDTH].reshape(B, T, N_DIFF_HEADS, 2, DIFF_HEAD_DIM)
    v = z[..., 2 * ATTN_WIDTH:3 * ATTN_WIDTH].reshape(B, T, N_DIFF_HEADS, V_HEAD_DIM)
    u = z[..., 3 * ATTN_WIDTH:]
    return rope_partial(q, pos), rope_partial(k, pos), v, u


def diff_lambda(lq1, lk1, lq2, lk2, lam_init):
    a = jnp.exp(jnp.sum(lq1.astype(F32) * lk1.astype(F32)))
    b = jnp.exp(jnp.sum(lq2.astype(F32) * lk2.astype(F32)))
    return a - b + lam_init


def diff_attn_prompt(q, k, v, lam):
    B, S = q.shape[:2]
    nb = S // Q_BLOCK
    scale = DIFF_HEAD_DIM ** -0.5
    qb = jnp.moveaxis(q.reshape(B, nb, Q_BLOCK, N_DIFF_HEADS, 2, DIFF_HEAD_DIM), 1, 0)
    kf = k.astype(F32)
    vf = v.astype(F32)
    k_pos = jnp.arange(S)

    def one_block(args):
        qblk, bi = args
        q_pos = bi * Q_BLOCK + jnp.arange(Q_BLOCK)
        s = jnp.einsum('bqhcd,bkhcd->bhcqk', qblk.astype(F32) * scale, kf)
        s = jnp.where(k_pos[None, :] <= q_pos[:, None], s, -jnp.inf)
        a = jax.nn.softmax(s, axis=-1)
        w = a[:, :, 0] - lam * a[:, :, 1]
        return jnp.einsum('bhqk,bkhe->bqhe', w, vf)

    o = lax.map(one_block, (qb, jnp.arange(nb)))
    return jnp.moveaxis(o, 0, 1).reshape(B, S, N_DIFF_HEADS, V_HEAD_DIM)


def _online_update(carry, s, v):
    m, l, acc = carry
    m_new = jnp.maximum(m, jnp.max(s, axis=-1))
    alpha = jnp.exp(m - m_new)
    p = jnp.exp(s - m_new[..., None])
    l = alpha * l + jnp.sum(p, axis=-1)
    acc = alpha[..., None] * acc + jnp.einsum('bhcqk,bkhe->bhcqe', p, v)
    return (m_new, l, acc)


def diff_attn_sample(q, k_new, v_new, cache_k, cache_v, page_table, layer, lam):
    DB, T = q.shape[:2]
    scale = DIFF_HEAD_DIM ** -0.5
    qf = q.astype(F32) * scale
    carry0 = (jnp.full((DB, N_DIFF_HEADS, 2, T), -jnp.inf, F32),
              jnp.zeros((DB, N_DIFF_HEADS, 2, T), F32),
              jnp.zeros((DB, N_DIFF_HEADS, 2, T, V_HEAD_DIM), F32))

    def page_step(carry, ids):
        kp = cache_k[layer, ids].astype(F32).reshape(DB, PAGE_SIZE, N_DIFF_HEADS, 2, DIFF_HEAD_DIM)
        vp = cache_v[layer, ids].astype(F32)
        s = jnp.einsum('bqhcd,bphcd->bhcqp', qf, kp)
        return _online_update(carry, s, vp), None

    carry, _ = lax.scan(page_step, carry0, page_table.T)
    s = jnp.einsum('bqhcd,bkhcd->bhcqk', qf, k_new.astype(F32))
    s = jnp.where(jnp.arange(T)[None, :] <= jnp.arange(T)[:, None], s, -jnp.inf)
    _, l, acc = _online_update(carry, s, v_new.astype(F32))
    o = acc / l[..., None]
    w = o[:, :, 0] - lam * o[:, :, 1]
    return jnp.transpose(w, (0, 2, 1, 3))


def attn_post(o, g_sub, lam_init):
    B, T = o.shape[:2]
    y = rmsnorm(o, g_sub) * (1.0 - lam_init)
    return y.reshape(B, T, ATTN_WIDTH)


def pool_mix(u_ext, pos, w_pool, scale):
    B = u_ext.shape[0]
    T = pos.shape[0]
    uf = u_ext.astype(F32)
    c = jnp.concatenate([jnp.zeros((B, 1, POOL_WIDTH), F32), lax.cumsum(uf, axis=1)], axis=1)
    c_now = c[:, POOL_CTX + 1:]
    outs = []
    for g, w in enumerate(POOL_WINDOWS):
        sl = slice(g * POOL_GROUP_WIDTH, (g + 1) * POOL_GROUP_WIDTH)
        c_prev = c[:, POOL_CTX + 1 - w:POOL_CTX + 1 - w + T, sl]
        cnt = jnp.minimum(w, pos + 1).astype(F32)[None, :, None]
        outs.append((c_now[..., sl] - c_prev) / cnt)
    z = jnp.concatenate(outs, axis=-1) - uf[:, POOL_CTX:]
    y = jnp.einsum('btgc,gcd->btgd', z.reshape(B, T, N_POOL_GROUPS, POOL_GROUP_WIDTH), w_pool.astype(F32))
    return y.reshape(B, T, POOL_WIDTH) * scale.astype(F32)


def mixer_out(o_attn, z_pool, w_out):
    return jnp.concatenate([o_attn.astype(w_out.dtype), z_pool.astype(w_out.dtype)], axis=-1) @ w_out


def hier_moe(n, w_rg, w_re, w_g, w_u, w_d):
    B, T, D = n.shape
    x = n.reshape(B * T, D)
    cg = (x @ w_rg).astype(F32)
    g_sel = jnp.argmax(cg, axis=-1)
    g_w = jnp.max(jax.nn.softmax(cg, axis=-1), axis=-1)
    fe = jnp.einsum('nd,dge->nge', x, w_re).astype(F32)
    fe_sel = jnp.take_along_axis(fe, g_sel[:, None, None], axis=1)[:, 0]
    top_v, top_i = lax.top_k(fe_sel, FINE_TOP_K)
    top_w = jax.nn.softmax(top_v, axis=-1) * g_w[:, None]
    e_w = jnp.sum(top_w[..., None] * jax.nn.one_hot(top_i, EXPERTS_PER_GROUP, dtype=F32), axis=1)
    comb = jax.nn.one_hot(g_sel, N_EXPERT_GROUPS, dtype=F32)[:, :, None] * e_w[:, None, :]
    out = jnp.zeros((B * T, D), F32)
    for gi in range(N_EXPERT_GROUPS):
        hid = jax.nn.silu(jnp.einsum('nd,edf->nef', x, w_g[gi])) * jnp.einsum('nd,edf->nef', x, w_u[gi])
        out = out + jnp.einsum('nef,efd->nd', hid * comb[:, gi, :, None], w_d[gi]).astype(F32)
    return out.reshape(B, T, D)


def channel_and_ple(h, p_l, g_ffn, w_rg, w_re, w_g, w_u, w_d, g_ple, w_pg, w_pp):
    h = h + hier_moe(rmsnorm(h, g_ffn), w_rg, w_re, w_g, w_u, w_d).astype(h.dtype)
    gate = jax.nn.sigmoid((rmsnorm(h, g_ple) @ w_pg).astype(F32))
    emb = (p_l @ w_pp).astype(F32)
    return h + (gate * emb).astype(h.dtype)


def setup_inputs(seed: int = 0) -> dict:
    key = jax.random.key(seed)
    ks = iter(jax.random.split(key, 40))
    n_pages = PAST_LEN // PAGE_SIZE
    n_used = DEC_BATCH * n_pages
    n_pool = n_used + max(1, n_used // 4)

    def nrm(shape, fan_in):
        return jax.random.normal(next(ks), shape, F32) * (fan_in ** -0.5)

    def gain(shape):
        return 1.0 + 0.05 * jax.random.normal(next(ks), shape, F32)

    page_table = jax.random.permutation(next(ks), n_pool)[:n_used].reshape(DEC_BATCH, n_pages).astype(jnp.int32)
    G, E, F = N_EXPERT_GROUPS, EXPERTS_PER_GROUP, EXPERT_FF
    return {
        'x_prompt': jax.random.normal(next(ks), (BATCH, SEQ, D_MODEL), F32),
        'x_sample': jax.random.normal(next(ks), (DEC_BATCH, DEC_SEQ, D_MODEL), F32),
        'cache_k': jax.random.normal(next(ks), (DEPTH, n_pool, PAGE_SIZE, N_DIFF_HEADS, 2 * DIFF_HEAD_DIM), F32),
        'cache_v': jax.random.normal(next(ks), (DEPTH, n_pool, PAGE_SIZE, N_DIFF_HEADS, V_HEAD_DIM), F32),
        'state_pool': jax.random.normal(next(ks), (DEPTH, DEC_BATCH, POOL_CTX, POOL_WIDTH), F32),
        'page_table': page_table,
        'p_prompt': jax.random.normal(next(ks), (DEPTH, BATCH, SEQ, PLE_DIM), F32),
        'p_sample': jax.random.normal(next(ks), (DEPTH, DEC_BATCH, DEC_SEQ, PLE_DIM), F32),
        'g_mix_norm': gain((DEPTH, D_MODEL)),
        'w_in': nrm((DEPTH, D_MODEL, IN_WIDTH), D_MODEL),
        'lambda_q1': 0.1 * jax.random.normal(next(ks), (DEPTH, DIFF_HEAD_DIM), F32),
        'lambda_k1': 0.1 * jax.random.normal(next(ks), (DEPTH, DIFF_HEAD_DIM), F32),
        'lambda_q2': 0.1 * jax.random.normal(next(ks), (DEPTH, DIFF_HEAD_DIM), F32),
        'lambda_k2': 0.1 * jax.random.normal(next(ks), (DEPTH, DIFF_HEAD_DIM), F32),
        'g_subln': gain((DEPTH, V_HEAD_DIM)),
        'w_pool': nrm((DEPTH, N_POOL_GROUPS, POOL_GROUP_WIDTH, POOL_GROUP_WIDTH), POOL_GROUP_WIDTH),
        'pool_scale': gain((DEPTH, POOL_WIDTH)),
        'w_out': nrm((DEPTH, MIX_WIDTH, D_MODEL), MIX_WIDTH),
        'g_ffn_norm': gain((DEPTH, D_MODEL)),
        'w_router_group': nrm((DEPTH, D_MODEL, G), D_MODEL),
        'w_router_expert': nrm((DEPTH, D_MODEL, G, E), D_MODEL),
        'w_gate_e': nrm((DEPTH, G, E, D_MODEL, F), D_MODEL),
        'w_up_e': nrm((DEPTH, G, E, D_MODEL, F), D_MODEL),
        'w_down_e': nrm((DEPTH, G, E, F, D_MODEL), F),
        'g_ple_norm': gain((DEPTH, D_MODEL)),
        'w_ple_gate': nrm((DEPTH, D_MODEL, D_MODEL), D_MODEL),
        'w_ple_proj': nrm((DEPTH, PLE_DIM, D_MODEL), PLE_DIM),
        'g_final': gain((D_MODEL,)),
    }


def reference(x_prompt, x_sample, cache_k, cache_v, state_pool, page_table, p_prompt, p_sample,
              g_mix_norm, w_in, lambda_q1, lambda_k1, lambda_q2, lambda_k2, g_subln, w_pool, pool_scale,
              w_out, g_ffn_norm, w_router_group, w_router_expert, w_gate_e, w_up_e, w_down_e,
              g_ple_norm, w_ple_gate, w_ple_proj, g_final):
    h_p, h_s = x_prompt, x_sample
    B, S = x_prompt.shape[:2]
    DB, T = x_sample.shape[:2]
    pos_p = jnp.arange(S)
    pos_s = PAST_LEN + jnp.arange(T)
    kp_l, vp_l, sp_l, ks_l, vs_l, ss_l = [], [], [], [], [], []
    for l in range(DEPTH):
        lam_init = 0.8 - 0.6 * math.exp(-0.3 * l)
        lam = diff_lambda(lambda_q1[l], lambda_k1[l], lambda_q2[l], lambda_k2[l], lam_init)
        ffn_args = (g_ffn_norm[l], w_router_group[l], w_router_expert[l], w_gate_e[l], w_up_e[l], w_down_e[l],
                    g_ple_norm[l], w_ple_gate[l], w_ple_proj[l])

        q, k, v, u = mix_inputs(rmsnorm(h_p, g_mix_norm[l]), w_in[l], pos_p)
        o = attn_post(diff_attn_prompt(q, k, v, lam), g_subln[l], lam_init)
        u_ext = jnp.concatenate([jnp.zeros((B, POOL_CTX, POOL_WIDTH), u.dtype), u], axis=1)
        z = pool_mix(u_ext, pos_p, w_pool[l], pool_scale[l])
        h_p = h_p + mixer_out(o, z, w_out[l]).astype(h_p.dtype)
        h_p = channel_and_ple(h_p, p_prompt[l], *ffn_args)
        kp_l.append(k.reshape(B, S, N_DIFF_HEADS, 2 * DIFF_HEAD_DIM))
        vp_l.append(v)
        sp_l.append(u_ext[:, -POOL_CTX:])

        q, k, v, u = mix_inputs(rmsnorm(h_s, g_mix_norm[l]), w_in[l], pos_s)
        o = attn_post(diff_attn_sample(q, k, v, cache_k, cache_v, page_table, l, lam), g_subln[l], lam_init)
        u_ext = jnp.concatenate([state_pool[l].astype(u.dtype), u], axis=1)
        z = pool_mix(u_ext, pos_s, w_pool[l], pool_scale[l])
        h_s = h_s + mixer_out(o, z, w_out[l]).astype(h_s.dtype)
        h_s = channel_and_ple(h_s, p_sample[l], *ffn_args)
        ks_l.append(k.reshape(DB, T, N_DIFF_HEADS, 2 * DIFF_HEAD_DIM))
        vs_l.append(v)
        ss_l.append(u_ext[:, -POOL_CTX:])

    y_prompt = rmsnorm(h_p, g_final)
    y_sample = rmsnorm(h_s, g_final)
    k_prompt_new = jnp.stack(kp_l, axis=0)
    v_prompt_new = jnp.stack(vp_l, axis=0)
    pool_prompt_new = jnp.stack(sp_l, axis=0)
    k_sample_new = jnp.stack(ks_l, axis=0)
    v_sample_new = jnp.stack(vs_l, axis=0)
    pool_sample_new = jnp.stack(ss_l, axis=0)
    return (y_prompt, y_sample, k_prompt_new, v_prompt_new, pool_prompt_new, k_sample_new, v_sample_new, pool_sample_new)
```

```python
import functools
import math

import jax
import jax.numpy as jnp
from jax import lax
from jax.experimental import pallas as pl
from jax.experimental.pallas import tpu as pltpu

F32 = jnp.float32
BF16 = jnp.bfloat16

LANES = 128
DIFF_HEAD_DIM = 64
ROT_DIM = DIFF_HEAD_DIM // 4
ROT_HALF = ROT_DIM // 2
ROPE_THETA = 500000.0
POOL_WINDOWS = (2, 4, 8, 16)
POOL_CTX = max(POOL_WINDOWS) - 1
HALO = 16
RMS_EPS = 1e-6
N_GROUPS = 4
N_PER_GROUP = 4
N_EXPERTS = N_GROUPS * N_PER_GROUP
EXPERT_LANE0 = N_GROUPS
NEG_INF = float("-inf")
VMEM_LIMIT = 56 * 1024 * 1024


def _rms(x, g):
    return x * lax.rsqrt(jnp.mean(x * x, axis=-1, keepdims=True) + RMS_EPS) * g


def _params(sem):
    return pltpu.CompilerParams(dimension_semantics=sem, vmem_limit_bytes=VMEM_LIMIT)


def _rope_tables(pos):
    inv = ROPE_THETA ** (-jnp.arange(0, ROT_DIM, 2, dtype=F32) / ROT_DIM)
    ang = pos.astype(F32)[:, None] * inv[None, :]
    cos, sin = jnp.cos(ang), jnp.sin(ang)
    lane = jnp.arange(LANES)
    r = lane % DIFF_HEAD_DIM
    i = r % ROT_HALF
    c = jnp.where(r < ROT_DIM, cos[:, i], 1.0)
    sa = jnp.where(r < ROT_HALF, -sin[:, i], 0.0)
    sb = jnp.where((r >= ROT_HALF) & (r < ROT_DIM), sin[:, i], 0.0)
    return c, sa, sb


def _pool_mix(ue_ref, u, pos0, w_pool_ref, scale_ref, zp_ref):
    tm = u.shape[0]
    pos = pos0 + lax.broadcasted_iota(jnp.int32, (tm, 1), 0)
    for g, w in enumerate(POOL_WINDOWS):
        cols = slice(g * LANES, (g + 1) * LANES)
        acc = u[:, cols]
        for k in range(1, w):
            acc = acc + ue_ref[HALO - k:HALO - k + tm, cols]
        cnt = jnp.minimum(w, pos + 1).astype(F32)
        z = acc / cnt - u[:, cols]
        y = jnp.dot(z.astype(BF16), w_pool_ref[g], preferred_element_type=F32)
        zp_ref[:, cols] = (y * scale_ref[:, cols]).astype(BF16)


def _inproj_kernel(x_ref, g_ref, w_ref, c_ref, sa_ref, sb_ref, *rest, pool, tiles_per_seq):
    if pool:
        w_pool_ref, scale_ref, q_ref, k_ref, v_ref, zp_ref, tail_ref, ue_ref = rest
    else:
        q_ref, k_ref, v_ref, u_ref = rest
    tm = x_ref.shape[0]
    aw = q_ref.shape[1]
    n = _rms(x_ref[...], g_ref[...])
    z = jnp.dot(n.astype(BF16), w_ref[...], preferred_element_type=F32)
    c, sa, sb = c_ref[...], sa_ref[...], sb_ref[...]

    def rope(blk):
        return blk * c + pltpu.roll(blk, LANES - ROT_HALF, 1) * sa + pltpu.roll(blk, ROT_HALF, 1) * sb

    q_scale = DIFF_HEAD_DIM ** -0.5
    for j in range(aw // LANES):
        cols = slice(j * LANES, (j + 1) * LANES)
        q_ref[:, cols] = (rope(z[:, j * LANES:(j + 1) * LANES]) * q_scale).astype(BF16)
        k_ref[:, cols] = rope(z[:, aw + j * LANES:aw + (j + 1) * LANES])
    v_ref[...] = z[:, 2 * aw:3 * aw]
    u = z[:, 3 * aw:]
    if not pool:
        u_ref[...] = u
        return
    t = pl.program_id(0) % tiles_per_seq

    @pl.when(t == 0)
    def _():
        ue_ref[0:HALO, :] = jnp.zeros((HALO, u.shape[1]), F32)

    @pl.when(t != 0)
    def _():
        ue_ref[0:HALO, :] = ue_ref[tm:tm + HALO, :]

    ue_ref[HALO:HALO + tm, :] = u
    tail_ref[0] = u[tm - HALO:, :]
    _pool_mix(ue_ref, u, t * tm, w_pool_ref, scale_ref, zp_ref)


def _inproj(x, g, w_in, tables, aw, tm, tiles_per_seq, pool_args=None):
    n_tok, d = x.shape
    in_w = w_in.shape[1]
    pw = in_w - 3 * aw
    pool = pool_args is not None
    grid = (n_tok // tm,)
    row = lambda i: (i, 0)
    fixed = lambda i: (0, 0)
    tab = lambda i: (i % tiles_per_seq, 0)
    in_specs = [pl.BlockSpec((tm, d), row), pl.BlockSpec((1, d), fixed), pl.BlockSpec((d, in_w), fixed),
                pl.BlockSpec((tm, LANES), tab), pl.BlockSpec((tm, LANES), tab), pl.BlockSpec((tm, LANES), tab)]
    args = [x, g, w_in, *tables]
    out_shape = [jax.ShapeDtypeStruct((n_tok, aw), BF16), jax.ShapeDtypeStruct((n_tok, aw), F32),
                 jax.ShapeDtypeStruct((n_tok, aw), F32)]
    out_specs = [pl.BlockSpec((tm, aw), row)] * 3
    scratch = []
    if pool:
        w_pool, scale = pool_args
        in_specs += [pl.BlockSpec(w_pool.shape, lambda i: (0, 0, 0)), pl.BlockSpec((1, pw), fixed)]
        args += [w_pool, scale]
        out_shape += [jax.ShapeDtypeStruct((n_tok, pw), BF16), jax.ShapeDtypeStruct((grid[0], HALO, pw), F32)]
        out_specs += [pl.BlockSpec((tm, pw), row), pl.BlockSpec((1, HALO, pw), lambda i: (i, 0, 0))]
        scratch = [pltpu.VMEM((tm + HALO, pw), F32)]
    else:
        out_shape += [jax.ShapeDtypeStruct((n_tok, pw), F32)]
        out_specs += [pl.BlockSpec((tm, pw), row)]
    return pl.pallas_call(
        functools.partial(_inproj_kernel, pool=pool, tiles_per_seq=tiles_per_seq),
        grid=grid, in_specs=in_specs, out_specs=out_specs, out_shape=out_shape, scratch_shapes=scratch,
        compiler_params=_params(("arbitrary",)), name="inproj_pool" if pool else "inproj",
    )(*args)


def _diff_lambda(lam_ref, lam_init):
    lv = lam_ref[...]
    a = jnp.exp(jnp.sum(lv[0:1] * lv[1:2], axis=-1, keepdims=True))
    b = jnp.exp(jnp.sum(lv[2:3] * lv[3:4], axis=-1, keepdims=True))
    return a - b + lam_init


def _sub_norm(w, g, lam_init):
    return _rms(w, g) * (1.0 - lam_init)


def _stack_maps(q):
    lane = lax.broadcasted_iota(jnp.int32, q.shape, 1)
    first = jnp.where(lane < DIFF_HEAD_DIM, 1.0, 0.0).astype(q.dtype)
    return jnp.concatenate([q * first, q * (1.0 - first)], axis=0)


def _attn_prompt_kernel(lam_ref, g_ref, q_ref, k_ref, v_ref, o_ref, kb_ref, vb_ref, m_ref, l_ref, acc_ref,
                        *, lam_init):
    tq = q_ref.shape[1]
    qi = pl.program_id(2)

    @pl.when(qi == 0)
    def _():
        kb_ref[...] = k_ref[0].astype(BF16)
        vb_ref[...] = v_ref[0].astype(BF16)

    qq = _stack_maps(q_ref[0])
    m_ref[...] = jnp.full(m_ref.shape, NEG_INF, F32)
    l_ref[...] = jnp.zeros(l_ref.shape, F32)
    acc_ref[...] = jnp.zeros(acc_ref.shape, F32)

    def block(j, masked):
        start = pl.multiple_of(j * tq, tq)
        kb = kb_ref[pl.ds(start, tq), :]
        vb = vb_ref[pl.ds(start, tq), :]
        s = lax.dot_general(qq, kb, (((1,), (1,)), ((), ())), preferred_element_type=F32)
        if masked:
            r = lax.broadcasted_iota(jnp.int32, s.shape, 0)
            r = jnp.where(r >= tq, r - tq, r)
            c = lax.broadcasted_iota(jnp.int32, s.shape, 1)
            s = jnp.where(c <= r, s, NEG_INF)
        m_prev = m_ref[...]
        m_new = jnp.maximum(m_prev, jnp.max(s, axis=1, keepdims=True))
        alpha = jnp.exp(m_prev - m_new)
        p = jnp.exp(s - m_new)
        l_ref[...] = alpha * l_ref[...] + jnp.sum(p, axis=1, keepdims=True)
        acc_ref[...] = alpha * acc_ref[...] + jnp.dot(p.astype(BF16), vb, preferred_element_type=F32)
        m_ref[...] = m_new

    def body(j, carry):
        block(j, masked=False)
        return carry

    lax.fori_loop(0, qi, body, 0)
    block(qi, masked=True)

    o = acc_ref[...] / l_ref[...]
    w = o[:tq] - _diff_lambda(lam_ref, lam_init) * o[tq:]
    o_ref[0] = _sub_norm(w, g_ref[...], lam_init).astype(BF16)


def _attn_prompt(q, k, v, lam_vecs, g_sub, lam_init, tq):
    b, s, aw = q.shape
    nh = aw // LANES
    grid = (b, nh, s // tq)
    fixed = lambda bi, h, qi: (0, 0)
    return pl.pallas_call(
        functools.partial(_attn_prompt_kernel, lam_init=lam_init),
        grid=grid,
        in_specs=[pl.BlockSpec(lam_vecs.shape, fixed), pl.BlockSpec(g_sub.shape, fixed),
                  pl.BlockSpec((1, tq, LANES), lambda bi, h, qi: (bi, qi, h)),
                  pl.BlockSpec((1, s, LANES), lambda bi, h, qi: (bi, 0, h)),
                  pl.BlockSpec((1, s, LANES), lambda bi, h, qi: (bi, 0, h))],
        out_specs=pl.BlockSpec((1, tq, LANES), lambda bi, h, qi: (bi, qi, h)),
        out_shape=jax.ShapeDtypeStruct((b, s, aw), BF16),
        scratch_shapes=[pltpu.VMEM((s, LANES), BF16), pltpu.VMEM((s, LANES), BF16),
                        pltpu.VMEM((2 * tq, 1), F32), pltpu.VMEM((2 * tq, 1), F32),
                        pltpu.VMEM((2 * tq, LANES), F32)],
        compiler_params=_params(("parallel", "parallel", "arbitrary")), name="attn_prompt",
    )(lam_vecs, g_sub, q, k, v)


def _attn_decode_kernel(pt_ref, lam_ref, g_ref, q_ref, kn_ref, vn_ref, *rest, lam_init, n_pg):
    k_refs, v_refs = rest[:n_pg], rest[n_pg:2 * n_pg]
    o_ref, m_ref, l_ref, acc_ref = rest[2 * n_pg:]
    ci = pl.program_id(1)
    aw = q_ref.shape[2]
    nmap = 2 * aw // LANES
    q = jnp.broadcast_to(q_ref[0].astype(F32), (nmap, aw))
    row = lax.broadcasted_iota(jnp.int32, (nmap, aw), 0)
    lane = lax.broadcasted_iota(jnp.int32, (nmap, aw), 1)
    qm_f32 = jnp.where(lane // DIFF_HEAD_DIM == row, q, 0.0)
    qm = qm_f32.astype(BF16)

    @pl.when(ci == 0)
    def _():
        m_ref[...] = jnp.sum(qm_f32 * kn_ref[0], axis=1, keepdims=True)
        l_ref[...] = jnp.ones(l_ref.shape, F32)
        acc_ref[...] = jnp.broadcast_to(vn_ref[0], acc_ref.shape)

    s = jnp.concatenate(
        [lax.dot_general(qm, kr[0].astype(BF16), (((1,), (1,)), ((), ())), preferred_element_type=F32)
         for kr in k_refs], axis=1)
    m_prev = m_ref[...]
    m_new = jnp.maximum(m_prev, jnp.max(s, axis=1, keepdims=True))
    alpha = jnp.exp(m_prev - m_new)
    p = jnp.exp(s - m_new)
    l_ref[...] = alpha * l_ref[...] + jnp.sum(p, axis=1, keepdims=True)
    pb = p.astype(BF16)
    page = k_refs[0].shape[1]
    pv = jnp.dot(pb[:, 0:page], v_refs[0][0].astype(BF16), preferred_element_type=F32)
    for i in range(1, n_pg):
        pv = pv + jnp.dot(pb[:, i * page:(i + 1) * page], v_refs[i][0].astype(BF16), preferred_element_type=F32)
    acc_ref[...] = alpha * acc_ref[...] + pv
    m_ref[...] = m_new

    @pl.when(ci == pl.num_programs(1) - 1)
    def _():
        o = acc_ref[...] / l_ref[...]
        lam = _diff_lambda(lam_ref, lam_init)
        for h in range(aw // LANES):
            cols = slice(h * LANES, (h + 1) * LANES)
            w = o[2 * h:2 * h + 1, cols] - lam * o[2 * h + 1:2 * h + 2, cols]
            o_ref[0, :, cols] = _sub_norm(w, g_ref[...], lam_init).astype(BF16)


def _attn_decode(q, k_new, v_new, cache_k, cache_v, page_table, layer, lam_vecs, g_sub, lam_init, n_pg):
    db, aw = q.shape
    _, n_pool, page, nh, hd = cache_k.shape
    n_pages = page_table.shape[1]
    ck = cache_k.reshape(-1, page, nh * hd)
    cv = cache_v.reshape(-1, page, nh * hd)
    pt = page_table.reshape(-1) + layer * n_pool
    fixed = lambda b, c, pt_ref: (0, 0)
    per_seq = lambda b, c, pt_ref: (b, 0, 0)

    def page_spec(i):
        return pl.BlockSpec((1, page, nh * hd), lambda b, c, pt_ref: (pt_ref[b * n_pages + c * n_pg + i], 0, 0))

    grid_spec = pltpu.PrefetchScalarGridSpec(
        num_scalar_prefetch=1, grid=(db, n_pages // n_pg),
        in_specs=[pl.BlockSpec(lam_vecs.shape, fixed), pl.BlockSpec(g_sub.shape, fixed),
                  pl.BlockSpec((1, 1, aw), per_seq), pl.BlockSpec((1, 1, aw), per_seq),
                  pl.BlockSpec((1, 1, aw), per_seq)]
        + [page_spec(i) for i in range(n_pg)] * 2,
        out_specs=pl.BlockSpec((1, 1, aw), per_seq),
        scratch_shapes=[pltpu.VMEM((2 * nh, 1), F32), pltpu.VMEM((2 * nh, 1), F32), pltpu.VMEM((2 * nh, aw), F32)])
    out = pl.pallas_call(
        functools.partial(_attn_decode_kernel, lam_init=lam_init, n_pg=n_pg),
        grid_spec=grid_spec, out_shape=jax.ShapeDtypeStruct((db, 1, aw), BF16),
        compiler_params=_params(("parallel", "arbitrary")), name="attn_decode",
    )(pt, lam_vecs, g_sub, q.reshape(db, 1, aw), k_new.reshape(db, 1, aw), v_new.reshape(db, 1, aw),
      *([ck] * n_pg), *([cv] * n_pg))
    return out.reshape(db, aw)


def _pool_sample_kernel(st_ref, u_ref, w_pool_ref, scale_ref, zp_ref, new_ref, *, pos):
    pw = u_ref.shape[1]
    u = u_ref[...]
    for g, w in enumerate(POOL_WINDOWS):
        cols = slice(g * LANES, (g + 1) * LANES)
        acc = u[:, cols]
        for k in range(1, w):
            r = POOL_CTX - k
            acc = acc + st_ref[:, r * pw + g * LANES:r * pw + (g + 1) * LANES]
        z = acc / float(min(w, pos + 1)) - u[:, cols]
        y = jnp.dot(z.astype(BF16), w_pool_ref[g], preferred_element_type=F32)
        zp_ref[:, cols] = (y * scale_ref[:, cols]).astype(BF16)
    new_ref[:, 0:(POOL_CTX - 1) * pw] = st_ref[:, pw:POOL_CTX * pw]
    new_ref[:, (POOL_CTX - 1) * pw:] = u


def _pool_sample(state, u, w_pool, scale, pos):
    db, ctx, pw = state.shape
    st2 = state.reshape(db, ctx * pw)
    zp, new = pl.pallas_call(
        functools.partial(_pool_sample_kernel, pos=pos),
        out_shape=[jax.ShapeDtypeStruct((db, pw), BF16), jax.ShapeDtypeStruct((db, ctx * pw), F32)],
        compiler_params=pltpu.CompilerParams(vmem_limit_bytes=VMEM_LIMIT), name="pool_sample",
    )(st2, u, w_pool, scale)
    return zp, new.reshape(db, ctx, pw)


def _route(lg):
    lane = lax.broadcasted_iota(jnp.int32, lg.shape, 1)
    big = jnp.int32(LANES)
    is_g = lane < N_GROUPS
    mg = jnp.max(jnp.where(is_g, lg, NEG_INF), axis=1, keepdims=True)
    g_sel = jnp.min(jnp.where(is_g & (lg == mg), lane, big), axis=1, keepdims=True)
    g_w = 1.0 / jnp.sum(jnp.where(is_g, jnp.exp(lg - mg), 0.0), axis=1, keepdims=True)
    lo = EXPERT_LANE0 + g_sel * N_PER_GROUP
    in_grp = (lane >= lo) & (lane < lo + N_PER_GROUP)
    v1 = jnp.max(jnp.where(in_grp, lg, NEG_INF), axis=1, keepdims=True)
    i1 = jnp.min(jnp.where(in_grp & (lg == v1), lane, big), axis=1, keepdims=True)
    rest = in_grp & (lane != i1)
    v2 = jnp.max(jnp.where(rest, lg, NEG_INF), axis=1, keepdims=True)
    i2 = jnp.min(jnp.where(rest & (lg == v2), lane, big), axis=1, keepdims=True)
    e2 = jnp.exp(v2 - v1)
    w1 = g_w / (1.0 + e2)
    w2 = g_w * e2 / (1.0 + e2)
    return jnp.where(lane == i1, w1, 0.0) + jnp.where(lane == i2, w2, 0.0)


def _mixout_kernel(x_ref, o_ref, zp_ref, wo_ref, g_ref, wr_ref, h_ref, n_ref, comb_ref):
    aw = o_ref.shape[1]
    mix = jnp.dot(o_ref[...], wo_ref[0:aw, :], preferred_element_type=F32)
    mix = mix + jnp.dot(zp_ref[...], wo_ref[aw:, :], preferred_element_type=F32)
    h = x_ref[...] + mix
    h_ref[...] = h
    n = _rms(h, g_ref[...])
    nb = n.astype(BF16)
    n_ref[...] = nb
    n_lo = (n - nb.astype(F32)).astype(BF16)
    w_hi, w_lo = wr_ref[0], wr_ref[1]
    lg = (jnp.dot(nb, w_hi, preferred_element_type=F32) + jnp.dot(n_lo, w_hi, preferred_element_type=F32)
          + jnp.dot(nb, w_lo, preferred_element_type=F32))
    comb_ref[...] = _route(lg)


def _mixout(x, o, zp, w_out, g_ffn, w_router, tm):
    n_tok, d = x.shape
    aw, pw = o.shape[1], zp.shape[1]
    row = lambda i: (i, 0)
    fixed = lambda i: (0, 0)
    return pl.pallas_call(
        _mixout_kernel, grid=(n_tok // tm,),
        in_specs=[pl.BlockSpec((tm, d), row), pl.BlockSpec((tm, aw), row), pl.BlockSpec((tm, pw), row),
                  pl.BlockSpec(w_out.shape, fixed), pl.BlockSpec((1, d), fixed),
                  pl.BlockSpec(w_router.shape, lambda i: (0, 0, 0))],
        out_specs=[pl.BlockSpec((tm, d), row), pl.BlockSpec((tm, d), row), pl.BlockSpec((tm, LANES), row)],
        out_shape=[jax.ShapeDtypeStruct((n_tok, d), F32), jax.ShapeDtypeStruct((n_tok, d), BF16),
                   jax.ShapeDtypeStruct((n_tok, LANES), F32)],
        compiler_params=_params(("parallel",)), name="mixout",
    )(x, o, zp, w_out, g_ffn, w_router)


def _experts_kernel(n_ref, comb_ref, h_ref, wg_ref, wu_ref, wd_ref, out_ref, acc_ref):
    e = pl.program_id(1)

    @pl.when(e == 0)
    def _():
        acc_ref[...] = h_ref[...]

    x = n_ref[...]
    comb = comb_ref[...]
    lane = lax.broadcasted_iota(jnp.int32, comb.shape, 1)
    c_e = jnp.sum(jnp.where(lane == EXPERT_LANE0 + e, comb, 0.0), axis=1, keepdims=True)
    gate = jnp.dot(x, wg_ref[0], preferred_element_type=F32)
    up = jnp.dot(x, wu_ref[0], preferred_element_type=F32)
    hid = gate * jax.nn.sigmoid(gate) * up * c_e
    acc_ref[...] += jnp.dot(hid.astype(BF16), wd_ref[0], preferred_element_type=F32)

    @pl.when(e == pl.num_programs(1) - 1)
    def _():
        out_ref[...] = acc_ref[...]


def _experts(n, comb, h, w_g, w_u, w_d, tm):
    n_tok, d = h.shape
    ne, _, ff = w_g.shape
    row = lambda i, e: (i, 0)
    return pl.pallas_call(
        _experts_kernel, grid=(n_tok // tm, ne),
        in_specs=[pl.BlockSpec((tm, d), row), pl.BlockSpec((tm, LANES), row), pl.BlockSpec((tm, d), row),
                  pl.BlockSpec((1, d, ff), lambda i, e: (e, 0, 0)), pl.BlockSpec((1, d, ff), lambda i, e: (e, 0, 0)),
                  pl.BlockSpec((1, ff, d), lambda i, e: (e, 0, 0))],
        out_specs=pl.BlockSpec((tm, d), row), out_shape=jax.ShapeDtypeStruct((n_tok, d), F32),
        scratch_shapes=[pltpu.VMEM((tm, d), F32)],
        compiler_params=_params(("parallel", "arbitrary")), name="experts",
    )(n, comb, h, w_g, w_u, w_d)


def _ple_kernel(h_ref, p_ref, g_ref, wpg_ref, wpp_ref, gf_ref, out_ref, *, final):
    h = h_ref[...]
    n = _rms(h, g_ref[...])
    gate = jax.nn.sigmoid(jnp.dot(n.astype(BF16), wpg_ref[...], preferred_element_type=F32))
    emb = jnp.dot(p_ref[...].astype(BF16), wpp_ref[...], preferred_element_type=F32)
    h = h + gate * emb
    out_ref[...] = _rms(h, gf_ref[...]) if final else h


def _ple(h, p, g_ple, w_pg, w_pp, g_final, final, tm):
    n_tok, d = h.shape
    pd = p.shape[1]
    row = lambda i: (i, 0)
    fixed = lambda i: (0, 0)
    return pl.pallas_call(
        functools.partial(_ple_kernel, final=final), grid=(n_tok // tm,),
        in_specs=[pl.BlockSpec((tm, d), row), pl.BlockSpec((tm, pd), row), pl.BlockSpec((1, d), fixed),
                  pl.BlockSpec((d, d), fixed), pl.BlockSpec((pd, d), fixed), pl.BlockSpec((1, d), fixed)],
        out_specs=pl.BlockSpec((tm, d), row), out_shape=jax.ShapeDtypeStruct((n_tok, d), F32),
        compiler_params=_params(("parallel",)), name="ple",
    )(h, p, g_ple, w_pg, w_pp, g_final)


def _token_tile(n_tok, want):
    tm = min(want, n_tok)
    assert n_tok % tm == 0, (n_tok, tm)
    return tm


def kernel(x_prompt, x_sample, cache_k, cache_v, state_pool, page_table, p_prompt, p_sample, g_mix_norm, w_in, lambda_q1, lambda_k1, lambda_q2, lambda_k2, g_subln, w_pool, pool_scale, w_out, g_ffn_norm, w_router_group, w_router_expert, w_gate_e, w_up_e, w_down_e, g_ple_norm, w_ple_gate, w_ple_proj, g_final):
    b, s, d = x_prompt.shape
    db, t_new, _ = x_sample.shape
    assert t_new == 1, "the decode kernel handles one new token per sequence"
    depth = w_in.shape[0]
    page = cache_k.shape[2]
    nh, hd = cache_k.shape[3], cache_k.shape[4]
    past_len = page_table.shape[1] * page
    pw = pool_scale.shape[1]
    aw = nh * hd
    ff = w_gate_e.shape[-1]

    tm_p = _token_tile(s, 512)
    tm_s = _token_tile(db, 128)
    tq = _token_tile(s, 256)
    tm_e = _token_tile(b * s, 1024)
    n_pg = 8
    assert page_table.shape[1] % n_pg == 0

    tab_p = _rope_tables(jnp.arange(s))
    tab_s = _rope_tables(jnp.full((tm_s,), past_len))

    h_p = x_prompt.reshape(b * s, d)
    h_s = x_sample.reshape(db, d)
    outs = [[] for _ in range(6)]
    for l in range(depth):
        lam_init = 0.8 - 0.6 * math.exp(-0.3 * l)
        lam_vecs = jnp.stack([lambda_q1[l], lambda_k1[l], lambda_q2[l], lambda_k2[l]])
        g_sub = g_subln[l][None]
        w_in_l = w_in[l].astype(BF16)
        w_pool_l = w_pool[l].astype(BF16)
        scale_l = pool_scale[l][None]
        w_out_l = w_out[l].astype(BF16)
        w_r = jnp.concatenate([w_router_group[l], w_router_expert[l].reshape(d, N_EXPERTS)], axis=1)
        w_r = jnp.pad(w_r, ((0, 0), (0, LANES - w_r.shape[1])))
        w_r_hi = w_r.astype(BF16)
        w_r = jnp.stack([w_r_hi, (w_r - w_r_hi.astype(F32)).astype(BF16)])
        w_g = w_gate_e[l].reshape(N_EXPERTS, d, ff).astype(BF16)
        w_u = w_up_e[l].reshape(N_EXPERTS, d, ff).astype(BF16)
        w_d = w_down_e[l].reshape(N_EXPERTS, ff, d).astype(BF16)
        w_pg = w_ple_gate[l].astype(BF16)
        w_pp = w_ple_proj[l].astype(BF16)
        final = l == depth - 1

        def channel(h, o, zp, p, tm_mix, tm_exp, tm_ple):
            h1, n2, comb = _mixout(h, o, zp, w_out_l, g_ffn_norm[l][None], w_r, tm_mix)
            h2 = _experts(n2, comb, h1, w_g, w_u, w_d, tm_exp)
            return _ple(h2, p, g_ple_norm[l][None], w_pg, w_pp, g_final[None], final, tm_ple)

        q, k, v, zp, tails = _inproj(h_p, g_mix_norm[l][None], w_in_l, tab_p, aw, tm_p, s // tm_p,
                                     pool_args=(w_pool_l, scale_l))
        o = _attn_prompt(q.reshape(b, s, aw), k.reshape(b, s, aw), v.reshape(b, s, aw),
                         lam_vecs, g_sub, lam_init, tq)
        h_p = channel(h_p, o.reshape(b * s, aw), zp, p_prompt[l].reshape(b * s, -1), tm_p, tm_e, tm_p)
        outs[0].append(k.reshape(b, s, nh, hd))
        outs[1].append(v.reshape(b, s, nh, hd))
        outs[2].append(tails.reshape(b, s // tm_p, HALO, pw)[:, -1, HALO - POOL_CTX:])

        q, k, v, u = _inproj(h_s, g_mix_norm[l][None], w_in_l, tab_s, aw, tm_s, 1)
        o = _attn_decode(q, k, v, cache_k, cache_v, page_table, l, lam_vecs, g_sub, lam_init, n_pg)
        zp, pool_new = _pool_sample(state_pool[l], u, w_pool_l, scale_l, past_len)
        h_s = channel(h_s, o, zp, p_sample[l].reshape(db, -1), tm_s, tm_s, tm_s)
        outs[3].append(k.reshape(db, 1, nh, hd))
        outs[4].append(v.reshape(db, 1, nh, hd))
        outs[5].append(pool_new)

    y_prompt = h_p.reshape(b, s, d)
    y_sample = h_s.reshape(db, 1, d)
    return (y_prompt, y_sample, *[jnp.stack(o, axis=0) for o in outs])
```

```python
import functools
import math

import jax
import jax.numpy as jnp
from jax import lax
from jax.experimental import pallas as pl
from jax.experimental.pallas import tpu as pltpu

F32 = jnp.float32
BF16 = jnp.bfloat16

LANES = 128
DIFF_HEAD_DIM = 64
ROT_DIM = DIFF_HEAD_DIM // 4
ROT_HALF = ROT_DIM // 2
ROPE_THETA = 500000.0
POOL_WINDOWS = (2, 4, 8, 16)
POOL_CTX = max(POOL_WINDOWS) - 1
HALO = 16
RMS_EPS = 1e-6
N_GROUPS = 4
N_PER_GROUP = 4
N_EXPERTS = N_GROUPS * N_PER_GROUP
EXPERT_LANE0 = N_GROUPS
NEG_INF = float("-inf")
ATTN_ROW_PARTS = 1
VMEM_LIMIT = 56 * 1024 * 1024


def _rms(x, g):
    return x * lax.rsqrt(jnp.mean(x * x, axis=-1, keepdims=True) + RMS_EPS) * g


def _params(sem):
    return pltpu.CompilerParams(dimension_semantics=sem, vmem_limit_bytes=VMEM_LIMIT)


def _rope_tables(pos):
    inv = ROPE_THETA ** (-jnp.arange(0, ROT_DIM, 2, dtype=F32) / ROT_DIM)
    ang = pos.astype(F32)[:, None] * inv[None, :]
    cos, sin = jnp.cos(ang), jnp.sin(ang)
    lane = jnp.arange(LANES)
    r = lane % DIFF_HEAD_DIM
    i = r % ROT_HALF
    c = jnp.where(r < ROT_DIM, cos[:, i], 1.0)
    sa = jnp.where(r < ROT_HALF, -sin[:, i], 0.0)
    sb = jnp.where((r >= ROT_HALF) & (r < ROT_DIM), sin[:, i], 0.0)
    return c, sa, sb


def _pool_mix(ue_ref, u, pos0, w_pool_ref, scale_ref, zp_ref):
    tm = u.shape[0]
    pos = pos0 + lax.broadcasted_iota(jnp.int32, (tm, 1), 0)
    for g, w in enumerate(POOL_WINDOWS):
        cols = slice(g * LANES, (g + 1) * LANES)
        acc = u[:, cols]
        for k in range(1, w):
            acc = acc + ue_ref[HALO - k:HALO - k + tm, cols]
        cnt = jnp.minimum(w, pos + 1).astype(F32)
        z = acc / cnt - u[:, cols]
        y = jnp.dot(z.astype(BF16), w_pool_ref[g], preferred_element_type=F32)
        zp_ref[:, cols] = (y * scale_ref[:, cols]).astype(BF16)


def _inproj_kernel(x_ref, g_ref, w_ref, c_ref, sa_ref, sb_ref, *rest, pool, tiles_per_seq):
    if pool:
        w_pool_ref, scale_ref, q_ref, k_ref, v_ref, zp_ref, tail_ref, ue_ref = rest
    else:
        q_ref, k_ref, v_ref, u_ref = rest
    tm = x_ref.shape[0]
    aw = q_ref.shape[1]
    n = _rms(x_ref[...], g_ref[...])
    z = jnp.dot(n.astype(BF16), w_ref[...], preferred_element_type=F32)
    c, sa, sb = c_ref[...], sa_ref[...], sb_ref[...]

    def rope(blk):
        return blk * c + pltpu.roll(blk, LANES - ROT_HALF, 1) * sa + pltpu.roll(blk, ROT_HALF, 1) * sb

    q_scale = DIFF_HEAD_DIM ** -0.5
    for j in range(aw // LANES):
        cols = slice(j * LANES, (j + 1) * LANES)
        q_ref[:, cols] = (rope(z[:, j * LANES:(j + 1) * LANES]) * q_scale).astype(BF16)
        k_ref[:, cols] = rope(z[:, aw + j * LANES:aw + (j + 1) * LANES])
    v_ref[...] = z[:, 2 * aw:3 * aw]
    u = z[:, 3 * aw:]
    if not pool:
        u_ref[...] = u
        return
    t = pl.program_id(0) % tiles_per_seq

    @pl.when(t == 0)
    def _():
        ue_ref[0:HALO, :] = jnp.zeros((HALO, u.shape[1]), F32)

    @pl.when(t != 0)
    def _():
        ue_ref[0:HALO, :] = ue_ref[tm:tm + HALO, :]

    ue_ref[HALO:HALO + tm, :] = u
    tail_ref[0] = u[tm - HALO:, :]
    _pool_mix(ue_ref, u, t * tm, w_pool_ref, scale_ref, zp_ref)


def _inproj(x, g, w_in, tables, aw, tm, tiles_per_seq, pool_args=None):
    n_tok, d = x.shape
    in_w = w_in.shape[1]
    pw = in_w - 3 * aw
    pool = pool_args is not None
    grid = (n_tok // tm,)
    row = lambda i: (i, 0)
    fixed = lambda i: (0, 0)
    tab = lambda i: (i % tiles_per_seq, 0)
    in_specs = [pl.BlockSpec((tm, d), row), pl.BlockSpec((1, d), fixed), pl.BlockSpec((d, in_w), fixed),
                pl.BlockSpec((tm, LANES), tab), pl.BlockSpec((tm, LANES), tab), pl.BlockSpec((tm, LANES), tab)]
    args = [x, g, w_in, *tables]
    out_shape = [jax.ShapeDtypeStruct((n_tok, aw), BF16), jax.ShapeDtypeStruct((n_tok, aw), F32),
                 jax.ShapeDtypeStruct((n_tok, aw), F32)]
    out_specs = [pl.BlockSpec((tm, aw), row)] * 3
    scratch = []
    if pool:
        w_pool, scale = pool_args
        in_specs += [pl.BlockSpec(w_pool.shape, lambda i: (0, 0, 0)), pl.BlockSpec((1, pw), fixed)]
        args += [w_pool, scale]
        out_shape += [jax.ShapeDtypeStruct((n_tok, pw), BF16), jax.ShapeDtypeStruct((grid[0], HALO, pw), F32)]
        out_specs += [pl.BlockSpec((tm, pw), row), pl.BlockSpec((1, HALO, pw), lambda i: (i, 0, 0))]
        scratch = [pltpu.VMEM((tm + HALO, pw), F32)]
    else:
        out_shape += [jax.ShapeDtypeStruct((n_tok, pw), F32)]
        out_specs += [pl.BlockSpec((tm, pw), row)]
    return pl.pallas_call(
        functools.partial(_inproj_kernel, pool=pool, tiles_per_seq=tiles_per_seq),
        grid=grid, in_specs=in_specs, out_specs=out_specs, out_shape=out_shape, scratch_shapes=scratch,
        compiler_params=_params(("arbitrary",)), name="inproj_pool" if pool else "inproj",
    )(*args)


def _diff_lambda(lam_ref, lam_init):
    lv = lam_ref[...]
    a = jnp.exp(jnp.sum(lv[0:1] * lv[1:2], axis=-1, keepdims=True))
    b = jnp.exp(jnp.sum(lv[2:3] * lv[3:4], axis=-1, keepdims=True))
    return a - b + lam_init


def _sub_norm(w, g, lam_init):
    return _rms(w, g) * (1.0 - lam_init)


def _stack_maps(q):
    lane = lax.broadcasted_iota(jnp.int32, q.shape, 1)
    first = jnp.where(lane < DIFF_HEAD_DIM, 1.0, 0.0).astype(q.dtype)
    return jnp.concatenate([q * first, q * (1.0 - first)], axis=0)


def _attn_prompt_kernel(lam_ref, g_ref, q_ref, k_ref, v_ref, o_ref, kb_ref, vb_ref, m_ref, acc_ref, s_ref,
                        *, lam_init):
    tq = q_ref.shape[1]
    qi = pl.program_id(2)

    @pl.when(qi == 0)
    def _():
        kb_ref[...] = k_ref[0].astype(BF16)
        vb_ref[:, 0:LANES] = v_ref[0].astype(BF16)
        vb_ref[:, LANES:] = jnp.ones((vb_ref.shape[0], LANES), BF16)

    qq = _stack_maps(q_ref[0])
    m_ref[...] = jnp.full(m_ref.shape, NEG_INF, F32)
    acc_ref[...] = jnp.zeros(acc_ref.shape, F32)

    def scores(j):
        start = pl.multiple_of(j * tq, tq)
        return lax.dot_general(qq, kb_ref[pl.ds(start, tq), :], (((1,), (1,)), ((), ())),
                               preferred_element_type=F32)

    def update(j, masked):
        start = pl.multiple_of(j * tq, tq)

        def load_scores():
            s = s_ref[...]
            if masked:
                r = lax.broadcasted_iota(jnp.int32, s.shape, 0)
                r = jnp.where(r >= tq, r - tq, r)
                c = lax.broadcasted_iota(jnp.int32, s.shape, 1)
                s = jnp.where(c <= r, s, NEG_INF)
            return s

        m_prev = m_ref[...]
        m_new = jnp.maximum(m_prev, jnp.max(load_scores(), axis=1, keepdims=True))
        alpha = jnp.exp(m_prev - m_new)
        p = jnp.exp(load_scores() - jnp.concatenate([m_new] * (tq // LANES), axis=1))
        pv = jnp.dot(p.astype(BF16), vb_ref[pl.ds(start, tq), :], preferred_element_type=F32)
        acc_ref[...] = jnp.concatenate([alpha, alpha], axis=1) * acc_ref[...] + pv
        m_ref[...] = m_new

    s_ref[...] = scores(0)

    def body(j, carry):
        update(j, masked=False)
        s_ref[...] = scores(j + 1)
        return carry

    lax.fori_loop(0, qi, body, 0)
    update(qi, masked=True)

    acc = acc_ref[...]
    o = acc[:, 0:LANES] / acc[:, LANES:]
    w = o[:tq] - _diff_lambda(lam_ref, lam_init) * o[tq:]
    o_ref[0] = _sub_norm(w, g_ref[...], lam_init).astype(BF16)


def _attn_prompt(q, k, v, lam_vecs, g_sub, lam_init, tq):
    b, s, aw = q.shape
    nh = aw // LANES
    grid = (b, nh, s // tq)
    fixed = lambda bi, h, qi: (0, 0)
    return pl.pallas_call(
        functools.partial(_attn_prompt_kernel, lam_init=lam_init),
        grid=grid,
        in_specs=[pl.BlockSpec(lam_vecs.shape, fixed), pl.BlockSpec(g_sub.shape, fixed),
                  pl.BlockSpec((1, tq, LANES), lambda bi, h, qi: (bi, qi, h)),
                  pl.BlockSpec((1, s, LANES), lambda bi, h, qi: (bi, 0, h)),
                  pl.BlockSpec((1, s, LANES), lambda bi, h, qi: (bi, 0, h))],
        out_specs=pl.BlockSpec((1, tq, LANES), lambda bi, h, qi: (bi, qi, h)),
        out_shape=jax.ShapeDtypeStruct((b, s, aw), BF16),
        scratch_shapes=[pltpu.VMEM((s, LANES), BF16), pltpu.VMEM((s, 2 * LANES), BF16),
                        pltpu.VMEM((2 * tq, LANES), F32), pltpu.VMEM((2 * tq, 2 * LANES), F32),
                        pltpu.VMEM((2 * tq, tq), F32)],
        compiler_params=_params(("parallel", "parallel", "arbitrary")), name="attn_prompt",
    )(lam_vecs, g_sub, q, k, v)


def _attn_decode_kernel(pt_ref, lam_ref, g_ref, q_ref, kn_ref, vn_ref, *rest, lam_init, n_pg):
    k_refs, v_refs = rest[:n_pg], rest[n_pg:2 * n_pg]
    o_ref, m_ref, l_ref, acc_ref = rest[2 * n_pg:]
    ci = pl.program_id(1)
    nh = q_ref.shape[1]
    nmap = 2 * nh

    def per_map(x):
        return jnp.concatenate([x[h:h + 1] for h in range(nh) for _ in range(2)], axis=0)

    row = lax.broadcasted_iota(jnp.int32, (nmap, LANES), 0)
    lane = lax.broadcasted_iota(jnp.int32, (nmap, LANES), 1)
    qm_f32 = jnp.where(lane // DIFF_HEAD_DIM == row % 2, per_map(q_ref[0].astype(F32)), 0.0)
    qm = qm_f32.astype(BF16)

    @pl.when(ci == 0)
    def _():
        m_ref[...] = jnp.sum(qm_f32 * per_map(kn_ref[0]), axis=1, keepdims=True)
        l_ref[...] = jnp.ones(l_ref.shape, F32)
        acc_ref[...] = per_map(vn_ref[0])

    s = jnp.concatenate(
        [lax.dot_general(qm, kr[0].astype(BF16), (((1,), (1,)), ((), ())), preferred_element_type=F32)
         for kr in k_refs], axis=1)
    srow = lax.broadcasted_iota(jnp.int32, s.shape, 0)
    scol = lax.broadcasted_iota(jnp.int32, s.shape, 1)
    s = jnp.where(scol % nh == srow // 2, s, NEG_INF)
    m_prev = m_ref[...]
    m_new = jnp.maximum(m_prev, jnp.max(s, axis=1, keepdims=True))
    alpha = jnp.exp(m_prev - m_new)
    p = jnp.exp(s - m_new)
    l_ref[...] = alpha * l_ref[...] + jnp.sum(p, axis=1, keepdims=True)
    pb = p.astype(BF16)
    rows = k_refs[0].shape[1]
    pv = jnp.dot(pb[:, 0:rows], v_refs[0][0].astype(BF16), preferred_element_type=F32)
    for i in range(1, n_pg):
        pv = pv + jnp.dot(pb[:, i * rows:(i + 1) * rows], v_refs[i][0].astype(BF16), preferred_element_type=F32)
    acc_ref[...] = alpha * acc_ref[...] + pv
    m_ref[...] = m_new

    @pl.when(ci == pl.num_programs(1) - 1)
    def _():
        o = acc_ref[...] / l_ref[...]
        lam = _diff_lambda(lam_ref, lam_init)
        for h in range(nh):
            w = o[2 * h:2 * h + 1] - lam * o[2 * h + 1:2 * h + 2]
            o_ref[0, :, h * LANES:(h + 1) * LANES] = _sub_norm(w, g_ref[...], lam_init).astype(BF16)


def _attn_decode(q, k_new, v_new, cache_k, cache_v, page_table, layer, lam_vecs, g_sub, lam_init, n_pg):
    db, aw = q.shape
    _, n_pool, page, nh, hd = cache_k.shape
    n_pages = page_table.shape[1]
    ck = cache_k.reshape(-1, page * nh, hd)
    cv = cache_v.reshape(-1, page * nh, hd)
    pt = page_table.reshape(-1) + layer * n_pool
    fixed = lambda b, c, pt_ref: (0, 0)
    per_seq = lambda b, c, pt_ref: (b, 0, 0)

    def page_spec(i):
        return pl.BlockSpec((1, page * nh, hd), lambda b, c, pt_ref: (pt_ref[b * n_pages + c * n_pg + i], 0, 0))

    grid_spec = pltpu.PrefetchScalarGridSpec(
        num_scalar_prefetch=1, grid=(db, n_pages // n_pg),
        in_specs=[pl.BlockSpec(lam_vecs.shape, fixed), pl.BlockSpec(g_sub.shape, fixed),
                  pl.BlockSpec((1, nh, hd), per_seq), pl.BlockSpec((1, nh, hd), per_seq),
                  pl.BlockSpec((1, nh, hd), per_seq)]
        + [page_spec(i) for i in range(n_pg)] * 2,
        out_specs=pl.BlockSpec((1, 1, aw), per_seq),
        scratch_shapes=[pltpu.VMEM((2 * nh, 1), F32), pltpu.VMEM((2 * nh, 1), F32), pltpu.VMEM((2 * nh, hd), F32)])
    out = pl.pallas_call(
        functools.partial(_attn_decode_kernel, lam_init=lam_init, n_pg=n_pg),
        grid_spec=grid_spec, out_shape=jax.ShapeDtypeStruct((db, 1, aw), BF16),
        compiler_params=_params(("parallel", "arbitrary")), name="attn_decode",
    )(pt, lam_vecs, g_sub, q.reshape(db, nh, hd), k_new.reshape(db, nh, hd), v_new.reshape(db, nh, hd),
      *([ck] * n_pg), *([cv] * n_pg))
    return out.reshape(db, aw)


def _pool_sample_kernel(st_ref, u_ref, w_pool_ref, scale_ref, zp_ref, new_ref, *, pos):
    pw = u_ref.shape[1]
    u = u_ref[...]
    for g, w in enumerate(POOL_WINDOWS):
        cols = slice(g * LANES, (g + 1) * LANES)
        acc = u[:, cols]
        for k in range(1, w):
            r = POOL_CTX - k
            acc = acc + st_ref[:, r * pw + g * LANES:r * pw + (g + 1) * LANES]
        z = acc / float(min(w, pos + 1)) - u[:, cols]
        y = jnp.dot(z.astype(BF16), w_pool_ref[g], preferred_element_type=F32)
        zp_ref[:, cols] = (y * scale_ref[:, cols]).astype(BF16)
    new_ref[:, 0:(POOL_CTX - 1) * pw] = st_ref[:, pw:POOL_CTX * pw]
    new_ref[:, (POOL_CTX - 1) * pw:] = u


def _pool_sample(state, u, w_pool, scale, pos):
    db, ctx, pw = state.shape
    st2 = state.reshape(db, ctx * pw)
    zp, new = pl.pallas_call(
        functools.partial(_pool_sample_kernel, pos=pos),
        out_shape=[jax.ShapeDtypeStruct((db, pw), BF16), jax.ShapeDtypeStruct((db, ctx * pw), F32)],
        compiler_params=pltpu.CompilerParams(vmem_limit_bytes=VMEM_LIMIT), name="pool_sample",
    )(st2, u, w_pool, scale)
    return zp, new.reshape(db, ctx, pw)


def _route(lg):
    lane = lax.broadcasted_iota(jnp.int32, lg.shape, 1)
    big = jnp.int32(LANES)
    is_g = lane < N_GROUPS
    mg = jnp.max(jnp.where(is_g, lg, NEG_INF), axis=1, keepdims=True)
    g_sel = jnp.min(jnp.where(is_g & (lg == mg), lane, big), axis=1, keepdims=True)
    g_w = 1.0 / jnp.sum(jnp.where(is_g, jnp.exp(lg - mg), 0.0), axis=1, keepdims=True)
    lo = EXPERT_LANE0 + g_sel * N_PER_GROUP
    in_grp = (lane >= lo) & (lane < lo + N_PER_GROUP)
    v1 = jnp.max(jnp.where(in_grp, lg, NEG_INF), axis=1, keepdims=True)
    i1 = jnp.min(jnp.where(in_grp & (lg == v1), lane, big), axis=1, keepdims=True)
    rest = in_grp & (lane != i1)
    v2 = jnp.max(jnp.where(rest, lg, NEG_INF), axis=1, keepdims=True)
    i2 = jnp.min(jnp.where(rest & (lg == v2), lane, big), axis=1, keepdims=True)
    e2 = jnp.exp(v2 - v1)
    w1 = g_w / (1.0 + e2)
    w2 = g_w * e2 / (1.0 + e2)
    return jnp.where(lane == i1, w1, 0.0) + jnp.where(lane == i2, w2, 0.0)


def _mixout_kernel(x_ref, o_ref, zp_ref, wo_ref, g_ref, wr_ref, h_ref, n_ref, comb_ref):
    aw = o_ref.shape[1]
    mix = jnp.dot(o_ref[...], wo_ref[0:aw, :], preferred_element_type=F32)
    mix = mix + jnp.dot(zp_ref[...], wo_ref[aw:, :], preferred_element_type=F32)
    h = x_ref[...] + mix
    h_ref[...] = h
    n = _rms(h, g_ref[...])
    nb = n.astype(BF16)
    n_ref[...] = nb
    n_lo = (n - nb.astype(F32)).astype(BF16)
    w_hi, w_lo = wr_ref[0], wr_ref[1]
    lg = (jnp.dot(nb, w_hi, preferred_element_type=F32) + jnp.dot(n_lo, w_hi, preferred_element_type=F32)
          + jnp.dot(nb, w_lo, preferred_element_type=F32))
    comb_ref[...] = _route(lg)


def _mixout(x, o, zp, w_out, g_ffn, w_router, tm):
    n_tok, d = x.shape
    aw, pw = o.shape[1], zp.shape[1]
    row = lambda i: (i, 0)
    fixed = lambda i: (0, 0)
    return pl.pallas_call(
        _mixout_kernel, grid=(n_tok // tm,),
        in_specs=[pl.BlockSpec((tm, d), row), pl.BlockSpec((tm, aw), row), pl.BlockSpec((tm, pw), row),
                  pl.BlockSpec(w_out.shape, fixed), pl.BlockSpec((1, d), fixed),
                  pl.BlockSpec(w_router.shape, lambda i: (0, 0, 0))],
        out_specs=[pl.BlockSpec((tm, d), row), pl.BlockSpec((tm, d), row), pl.BlockSpec((tm, LANES), row)],
        out_shape=[jax.ShapeDtypeStruct((n_tok, d), F32), jax.ShapeDtypeStruct((n_tok, d), BF16),
                   jax.ShapeDtypeStruct((n_tok, LANES), F32)],
        compiler_params=_params(("parallel",)), name="mixout",
    )(x, o, zp, w_out, g_ffn, w_router)


def _experts_kernel(n_ref, comb_ref, h_ref, wg_ref, wu_ref, wd_ref, out_ref, acc_ref):
    e = pl.program_id(1)

    @pl.when(e == 0)
    def _():
        acc_ref[...] = h_ref[...]

    x = n_ref[...]
    comb = comb_ref[...]
    lane = lax.broadcasted_iota(jnp.int32, comb.shape, 1)
    c_e = jnp.sum(jnp.where(lane == EXPERT_LANE0 + e, comb, 0.0), axis=1, keepdims=True)
    gate = jnp.dot(x, wg_ref[0], preferred_element_type=F32)
    up = jnp.dot(x, wu_ref[0], preferred_element_type=F32)
    hid = gate * jax.nn.sigmoid(gate) * up * c_e
    acc_ref[...] += jnp.dot(hid.astype(BF16), wd_ref[0], preferred_element_type=F32)

    @pl.when(e == pl.num_programs(1) - 1)
    def _():
        out_ref[...] = acc_ref[...]


def _experts(n, comb, h, w_g, w_u, w_d, tm):
    n_tok, d = h.shape
    ne, _, ff = w_g.shape
    row = lambda i, e: (i, 0)
    return pl.pallas_call(
        _experts_kernel, grid=(n_tok // tm, ne),
        in_specs=[pl.BlockSpec((tm, d), row), pl.BlockSpec((tm, LANES), row), pl.BlockSpec((tm, d), row),
                  pl.BlockSpec((1, d, ff), lambda i, e: (e, 0, 0)), pl.BlockSpec((1, d, ff), lambda i, e: (e, 0, 0)),
                  pl.BlockSpec((1, ff, d), lambda i, e: (e, 0, 0))],
        out_specs=pl.BlockSpec((tm, d), row), out_shape=jax.ShapeDtypeStruct((n_tok, d), F32),
        scratch_shapes=[pltpu.VMEM((tm, d), F32)],
        compiler_params=_params(("parallel", "arbitrary")), name="experts",
    )(n, comb, h, w_g, w_u, w_d)


def _ple_kernel(h_ref, p_ref, g_ref, wpg_ref, wpp_ref, gf_ref, out_ref, *, final):
    h = h_ref[...]
    n = _rms(h, g_ref[...])
    gate = jax.nn.sigmoid(jnp.dot(n.astype(BF16), wpg_ref[...], preferred_element_type=F32))
    emb = jnp.dot(p_ref[...].astype(BF16), wpp_ref[...], preferred_element_type=F32)
    h = h + gate * emb
    out_ref[...] = _rms(h, gf_ref[...]) if final else h


def _ple(h, p, g_ple, w_pg, w_pp, g_final, final, tm):
    n_tok, d = h.shape
    pd = p.shape[1]
    row = lambda i: (i, 0)
    fixed = lambda i: (0, 0)
    return pl.pallas_call(
        functools.partial(_ple_kernel, final=final), grid=(n_tok // tm,),
        in_specs=[pl.BlockSpec((tm, d), row), pl.BlockSpec((tm, pd), row), pl.BlockSpec((1, d), fixed),
                  pl.BlockSpec((d, d), fixed), pl.BlockSpec((pd, d), fixed), pl.BlockSpec((1, d), fixed)],
        out_specs=pl.BlockSpec((tm, d), row), out_shape=jax.ShapeDtypeStruct((n_tok, d), F32),
        compiler_params=_params(("parallel",)), name="ple",
    )(h, p, g_ple, w_pg, w_pp, g_final)


def _token_tile(n_tok, want):
    tm = min(want, n_tok)
    assert n_tok % tm == 0, (n_tok, tm)
    return tm


def kernel(x_prompt, x_sample, cache_k, cache_v, state_pool, page_table, p_prompt, p_sample, g_mix_norm, w_in, lambda_q1, lambda_k1, lambda_q2, lambda_k2, g_subln, w_pool, pool_scale, w_out, g_ffn_norm, w_router_group, w_router_expert, w_gate_e, w_up_e, w_down_e, g_ple_norm, w_ple_gate, w_ple_proj, g_final):
    b, s, d = x_prompt.shape
    db, t_new, _ = x_sample.shape
    assert t_new == 1, "the decode kernel handles one new token per sequence"
    depth = w_in.shape[0]
    page = cache_k.shape[2]
    nh, hd = cache_k.shape[3], cache_k.shape[4]
    past_len = page_table.shape[1] * page
    pw = pool_scale.shape[1]
    aw = nh * hd
    ff = w_gate_e.shape[-1]

    tm_p = _token_tile(s, 512)
    tm_s = _token_tile(db, 128)
    tq = _token_tile(s, 512)
    tm_e = _token_tile(b * s, 1024)
    n_pg = 16
    assert page_table.shape[1] % n_pg == 0

    tab_p = _rope_tables(jnp.arange(s))
    tab_s = _rope_tables(jnp.full((tm_s,), past_len))

    h_p = x_prompt.reshape(b * s, d)
    h_s = x_sample.reshape(db, d)
    outs = [[] for _ in range(6)]
    for l in range(depth):
        lam_init = 0.8 - 0.6 * math.exp(-0.3 * l)
        lam_vecs = jnp.stack([lambda_q1[l], lambda_k1[l], lambda_q2[l], lambda_k2[l]])
        g_sub = g_subln[l][None]
        w_in_l = w_in[l].astype(BF16)
        w_pool_l = w_pool[l].astype(BF16)
        scale_l = pool_scale[l][None]
        w_out_l = w_out[l].astype(BF16)
        w_r = jnp.concatenate([w_router_group[l], w_router_expert[l].reshape(d, N_EXPERTS)], axis=1)
        w_r = jnp.pad(w_r, ((0, 0), (0, LANES - w_r.shape[1])))
        w_r_hi = w_r.astype(BF16)
        w_r = jnp.stack([w_r_hi, (w_r - w_r_hi.astype(F32)).astype(BF16)])
        w_g = w_gate_e[l].reshape(N_EXPERTS, d, ff).astype(BF16)
        w_u = w_up_e[l].reshape(N_EXPERTS, d, ff).astype(BF16)
        w_d = w_down_e[l].reshape(N_EXPERTS, ff, d).astype(BF16)
        w_pg = w_ple_gate[l].astype(BF16)
        w_pp = w_ple_proj[l].astype(BF16)
        final = l == depth - 1

        def channel(h, o, zp, p, tm_mix, tm_exp, tm_ple):
            h1, n2, comb = _mixout(h, o, zp, w_out_l, g_ffn_norm[l][None], w_r, tm_mix)
            h2 = _experts(n2, comb, h1, w_g, w_u, w_d, tm_exp)
            return _ple(h2, p, g_ple_norm[l][None], w_pg, w_pp, g_final[None], final, tm_ple)

        q, k, v, zp, tails = _inproj(h_p, g_mix_norm[l][None], w_in_l, tab_p, aw, tm_p, s // tm_p,
                                     pool_args=(w_pool_l, scale_l))
        o = _attn_prompt(q.reshape(b, s, aw), k.reshape(b, s, aw), v.reshape(b, s, aw),
                         lam_vecs, g_sub, lam_init, tq)
        h_p = channel(h_p, o.reshape(b * s, aw), zp, p_prompt[l].reshape(b * s, -1), tm_p, tm_e, tm_p)
        outs[0].append(k.reshape(b, s, nh, hd))
        outs[1].append(v.reshape(b, s, nh, hd))
        outs[2].append(tails.reshape(b, s // tm_p, HALO, pw)[:, -1, HALO - POOL_CTX:])

        q, k, v, u = _inproj(h_s, g_mix_norm[l][None], w_in_l, tab_s, aw, tm_s, 1)
        o = _attn_decode(q, k, v, cache_k, cache_v, page_table, l, lam_vecs, g_sub, lam_init, n_pg)
        zp, pool_new = _pool_sample(state_pool[l], u, w_pool_l, scale_l, past_len)
        h_s = channel(h_s, o, zp, p_sample[l].reshape(db, -1), tm_s, tm_s, tm_s)
        outs[3].append(k.reshape(db, 1, nh, hd))
        outs[4].append(v.reshape(db, 1, nh, hd))
        outs[5].append(pool_new)

    y_prompt = h_p.reshape(b, s, d)
    y_sample = h_s.reshape(db, 1, d)
    return (y_prompt, y_sample, *[jnp.stack(o, axis=0) for o in outs])
```

```python
import functools
import math

import jax
import jax.numpy as jnp
from jax import lax
from jax.experimental import pallas as pl
from jax.experimental.pallas import tpu as pltpu

F32 = jnp.float32
BF16 = jnp.bfloat16

LANES = 128
DIFF_HEAD_DIM = 64
ROT_DIM = DIFF_HEAD_DIM // 4
ROT_HALF = ROT_DIM // 2
ROPE_THETA = 500000.0
POOL_WINDOWS = (2, 4, 8, 16)
POOL_CTX = max(POOL_WINDOWS) - 1
HALO = 16
RMS_EPS = 1e-6
N_GROUPS = 4
N_PER_GROUP = 4
N_EXPERTS = N_GROUPS * N_PER_GROUP
EXPERT_LANE0 = N_GROUPS
NEG_INF = float("-inf")
ATTN_HEADS_PER_STEP = 2
VMEM_LIMIT = 56 * 1024 * 1024


def _rms(x, g):
    return x * lax.rsqrt(jnp.mean(x * x, axis=-1, keepdims=True) + RMS_EPS) * g


def _params(sem):
    return pltpu.CompilerParams(dimension_semantics=sem, vmem_limit_bytes=VMEM_LIMIT)


def _rope_tables(pos):
    inv = ROPE_THETA ** (-jnp.arange(0, ROT_DIM, 2, dtype=F32) / ROT_DIM)
    ang = pos.astype(F32)[:, None] * inv[None, :]
    cos, sin = jnp.cos(ang), jnp.sin(ang)
    lane = jnp.arange(LANES)
    r = lane % DIFF_HEAD_DIM
    i = r % ROT_HALF
    c = jnp.where(r < ROT_DIM, cos[:, i], 1.0)
    sa = jnp.where(r < ROT_HALF, -sin[:, i], 0.0)
    sb = jnp.where((r >= ROT_HALF) & (r < ROT_DIM), sin[:, i], 0.0)
    return c, sa, sb


def _pool_mix(ue_ref, u, pos0, w_pool_ref, scale_ref, zp_ref):
    tm = u.shape[0]
    pos = pos0 + lax.broadcasted_iota(jnp.int32, (tm, 1), 0)
    for g, w in enumerate(POOL_WINDOWS):
        cols = slice(g * LANES, (g + 1) * LANES)
        acc = u[:, cols]
        for k in range(1, w):
            acc = acc + ue_ref[HALO - k:HALO - k + tm, cols]
        cnt = jnp.minimum(w, pos + 1).astype(F32)
        z = acc / cnt - u[:, cols]
        y = jnp.dot(z.astype(BF16), w_pool_ref[g], preferred_element_type=F32)
        zp_ref[:, cols] = (y * scale_ref[:, cols]).astype(BF16)


def _inproj_kernel(x_ref, g_ref, w_ref, c_ref, sa_ref, sb_ref, *rest, pool, tiles_per_seq):
    if pool:
        w_pool_ref, scale_ref, q_ref, k_ref, v_ref, kb_ref, vb_ref, zp_ref, tail_ref, ue_ref = rest
    else:
        q_ref, k_ref, v_ref, u_ref = rest
    tm = x_ref.shape[0]
    aw = q_ref.shape[1]
    nh = aw // LANES
    n = _rms(x_ref[...], g_ref[...])
    z = jnp.dot(n.astype(BF16), w_ref[...], preferred_element_type=F32)
    c, sa, sb = c_ref[...], sa_ref[...], sb_ref[...]

    def rope(blk):
        return blk * c + pltpu.roll(blk, LANES - ROT_HALF, 1) * sa + pltpu.roll(blk, ROT_HALF, 1) * sb

    q_scale = DIFF_HEAD_DIM ** -0.5
    for j in range(nh):
        cols = slice(j * LANES, (j + 1) * LANES)
        q_ref[:, cols] = (rope(z[:, j * LANES:(j + 1) * LANES]) * q_scale).astype(BF16)
        k = rope(z[:, aw + j * LANES:aw + (j + 1) * LANES])
        v = z[:, 2 * aw + j * LANES:2 * aw + (j + 1) * LANES]
        k_ref[pl.ds(j, tm, stride=nh), :] = k
        v_ref[pl.ds(j, tm, stride=nh), :] = v
        if pool:
            kb_ref[:, cols] = k.astype(BF16)
            vb_ref[:, cols] = v.astype(BF16)
    u = z[:, 3 * aw:]
    if not pool:
        u_ref[...] = u
        return
    t = pl.program_id(0) % tiles_per_seq

    @pl.when(t == 0)
    def _():
        ue_ref[0:HALO, :] = jnp.zeros((HALO, u.shape[1]), F32)

    @pl.when(t != 0)
    def _():
        ue_ref[0:HALO, :] = ue_ref[tm:tm + HALO, :]

    ue_ref[HALO:HALO + tm, :] = u
    tail_ref[0] = u[tm - HALO:, :]
    _pool_mix(ue_ref, u, t * tm, w_pool_ref, scale_ref, zp_ref)


def _inproj(x, g, w_in, tables, aw, tm, tiles_per_seq, pool_args=None):
    n_tok, d = x.shape
    in_w = w_in.shape[1]
    pw = in_w - 3 * aw
    nh = aw // LANES
    pool = pool_args is not None
    grid = (n_tok // tm,)
    row = lambda i: (i, 0)
    fixed = lambda i: (0, 0)
    tab = lambda i: (i % tiles_per_seq, 0)
    in_specs = [pl.BlockSpec((tm, d), row), pl.BlockSpec((1, d), fixed), pl.BlockSpec((d, in_w), fixed),
                pl.BlockSpec((tm, LANES), tab), pl.BlockSpec((tm, LANES), tab), pl.BlockSpec((tm, LANES), tab)]
    args = [x, g, w_in, *tables]
    out_shape = [jax.ShapeDtypeStruct((n_tok, aw), BF16), jax.ShapeDtypeStruct((n_tok * nh, LANES), F32),
                 jax.ShapeDtypeStruct((n_tok * nh, LANES), F32)]
    out_specs = [pl.BlockSpec((tm, aw), row), pl.BlockSpec((tm * nh, LANES), row), pl.BlockSpec((tm * nh, LANES), row)]
    scratch = []
    if pool:
        w_pool, scale = pool_args
        in_specs += [pl.BlockSpec(w_pool.shape, lambda i: (0, 0, 0)), pl.BlockSpec((1, pw), fixed)]
        args += [w_pool, scale]
        out_shape += [jax.ShapeDtypeStruct((n_tok, aw), BF16), jax.ShapeDtypeStruct((n_tok, aw), BF16),
                      jax.ShapeDtypeStruct((n_tok, pw), BF16), jax.ShapeDtypeStruct((grid[0], HALO, pw), F32)]
        out_specs += [pl.BlockSpec((tm, aw), row), pl.BlockSpec((tm, aw), row),
                      pl.BlockSpec((tm, pw), row), pl.BlockSpec((1, HALO, pw), lambda i: (i, 0, 0))]
        scratch = [pltpu.VMEM((tm + HALO, pw), F32)]
    else:
        out_shape += [jax.ShapeDtypeStruct((n_tok, pw), F32)]
        out_specs += [pl.BlockSpec((tm, pw), row)]
    return pl.pallas_call(
        functools.partial(_inproj_kernel, pool=pool, tiles_per_seq=tiles_per_seq),
        grid=grid, in_specs=in_specs, out_specs=out_specs, out_shape=out_shape, scratch_shapes=scratch,
        compiler_params=_params(("arbitrary",)), name="inproj_pool" if pool else "inproj",
    )(*args)


def _diff_lambda(lam_ref, lam_init):
    lv = lam_ref[...]
    a = jnp.exp(jnp.sum(lv[0:1] * lv[1:2], axis=-1, keepdims=True))
    b = jnp.exp(jnp.sum(lv[2:3] * lv[3:4], axis=-1, keepdims=True))
    return a - b + lam_init


def _sub_norm(w, g, lam_init):
    return _rms(w, g) * (1.0 - lam_init)


def _stack_maps(q):
    lane = lax.broadcasted_iota(jnp.int32, q.shape, 1)
    first = jnp.where(lane < DIFF_HEAD_DIM, 1.0, 0.0).astype(q.dtype)
    return jnp.concatenate([q * first, q * (1.0 - first)], axis=0)


def _attn_prompt_kernel(lam_ref, g_ref, q_ref, k_ref, v_ref, o_ref, kb_ref, vb_ref, m_ref, acc_ref, s_ref,
                        *, lam_init):
    tq = o_ref.shape[1]
    nhs = o_ref.shape[2] // LANES
    qi = pl.program_id(2)
    nq = pl.num_programs(2)

    def stacked_q(blk):
        rows = pl.ds(pl.multiple_of(blk * tq, tq), tq)
        return [_stack_maps(q_ref[0, rows, h * LANES:(h + 1) * LANES]) for h in range(nhs)]

    def scores_of(qs, h, j):
        start = pl.multiple_of(j * tq, tq)
        return lax.dot_general(qs[h], kb_ref[h, pl.ds(start, tq), :], (((1,), (1,)), ((), ())),
                               preferred_element_type=F32)

    qq = stacked_q(qi)
    scores = functools.partial(scores_of, qq)

    @pl.when(qi == 0)
    def _():
        for h in range(nhs):
            cols = slice(h * LANES, (h + 1) * LANES)
            kb_ref[h] = k_ref[0, :, cols]
            vb_ref[h, :, 0:LANES] = v_ref[0, :, cols]
            vb_ref[h, :, LANES:] = jnp.ones((vb_ref.shape[1], LANES), BF16)
        for h in range(nhs):
            s_ref[h] = scores(h, 0)

    m_ref[...] = jnp.full(m_ref.shape, NEG_INF, F32)
    acc_ref[...] = jnp.zeros(acc_ref.shape, F32)

    def update(h, j, masked):
        start = pl.multiple_of(j * tq, tq)

        def load_scores():
            s = s_ref[h]
            if masked:
                r = lax.broadcasted_iota(jnp.int32, s.shape, 0)
                r = jnp.where(r >= tq, r - tq, r)
                c = lax.broadcasted_iota(jnp.int32, s.shape, 1)
                s = jnp.where(c <= r, s, NEG_INF)
            return s

        m_prev = m_ref[h]
        m_new = jnp.maximum(m_prev, jnp.max(load_scores(), axis=1, keepdims=True))
        alpha = jnp.exp(m_prev - m_new)
        p = jnp.exp(load_scores() - jnp.concatenate([m_new] * (tq // LANES), axis=1))
        pv = jnp.dot(p.astype(BF16), vb_ref[h, pl.ds(start, tq), :], preferred_element_type=F32)
        acc_ref[h] = jnp.concatenate([alpha, alpha], axis=1) * acc_ref[h] + pv
        m_ref[h] = m_new

    def body(j, carry):
        for h in range(nhs):
            update(h, j, masked=False)
            s_ref[h] = scores(h, j + 1)
        return carry

    lax.fori_loop(0, qi, body, 0)
    lam = _diff_lambda(lam_ref, lam_init)
    q_next = stacked_q(jnp.minimum(qi + 1, nq - 1))
    for h in range(nhs):
        update(h, qi, masked=True)
        s_ref[h] = scores_of(q_next, h, 0)
        acc = acc_ref[h]
        o = acc[:, 0:LANES] / acc[:, LANES:]
        w = o[:tq] - lam * o[tq:]
        o_ref[0, :, h * LANES:(h + 1) * LANES] = _sub_norm(w, g_ref[...], lam_init).astype(BF16)


def _attn_prompt(q, k, v, lam_vecs, g_sub, lam_init, tq):
    b, s, aw = q.shape
    nhs = ATTN_HEADS_PER_STEP
    hw = nhs * LANES
    grid = (b, aw // hw, s // tq)
    fixed = lambda bi, h, qi: (0, 0)
    return pl.pallas_call(
        functools.partial(_attn_prompt_kernel, lam_init=lam_init),
        grid=grid,
        in_specs=[pl.BlockSpec(lam_vecs.shape, fixed), pl.BlockSpec(g_sub.shape, fixed),
                  pl.BlockSpec((1, s, hw), lambda bi, h, qi: (bi, 0, h)),
                  pl.BlockSpec((1, s, hw), lambda bi, h, qi: (bi, 0, h)),
                  pl.BlockSpec((1, s, hw), lambda bi, h, qi: (bi, 0, h))],
        out_specs=pl.BlockSpec((1, tq, hw), lambda bi, h, qi: (bi, qi, h)),
        out_shape=jax.ShapeDtypeStruct((b, s, aw), BF16),
        scratch_shapes=[pltpu.VMEM((nhs, s, LANES), BF16), pltpu.VMEM((nhs, s, 2 * LANES), BF16),
                        pltpu.VMEM((nhs, 2 * tq, LANES), F32), pltpu.VMEM((nhs, 2 * tq, 2 * LANES), F32),
                        pltpu.VMEM((nhs, 2 * tq, tq), F32)],
        compiler_params=_params(("parallel", "parallel", "arbitrary")), name="attn_prompt",
    )(lam_vecs, g_sub, q, k, v)


def _attn_decode_kernel(pt_ref, lam_ref, g_ref, q_ref, kn_ref, vn_ref, *rest, lam_init, n_pg):
    k_refs, v_refs = rest[:n_pg], rest[n_pg:2 * n_pg]
    o_ref, m_ref, l_ref, acc_ref = rest[2 * n_pg:]
    ci = pl.program_id(1)
    nh = q_ref.shape[1]
    nmap = 2 * nh

    def per_map(x):
        return jnp.concatenate([x[h:h + 1] for h in range(nh) for _ in range(2)], axis=0)

    row = lax.broadcasted_iota(jnp.int32, (nmap, LANES), 0)
    lane = lax.broadcasted_iota(jnp.int32, (nmap, LANES), 1)
    qm_f32 = jnp.where(lane // DIFF_HEAD_DIM == row % 2, per_map(q_ref[0].astype(F32)), 0.0)
    qm = qm_f32.astype(BF16)

    @pl.when(ci == 0)
    def _():
        m_ref[...] = jnp.sum(qm_f32 * per_map(kn_ref[0]), axis=1, keepdims=True)
        l_ref[...] = jnp.ones(l_ref.shape, F32)
        acc_ref[...] = per_map(vn_ref[0])

    s = jnp.concatenate(
        [lax.dot_general(qm, kr[0].astype(BF16), (((1,), (1,)), ((), ())), preferred_element_type=F32)
         for kr in k_refs], axis=1)
    srow = lax.broadcasted_iota(jnp.int32, s.shape, 0)
    scol = lax.broadcasted_iota(jnp.int32, s.shape, 1)
    s = jnp.where(scol % nh == srow // 2, s, NEG_INF)
    m_prev = m_ref[...]
    m_new = jnp.maximum(m_prev, jnp.max(s, axis=1, keepdims=True))
    alpha = jnp.exp(m_prev - m_new)
    p = jnp.exp(s - m_new)
    l_ref[...] = alpha * l_ref[...] + jnp.sum(p, axis=1, keepdims=True)
    pb = p.astype(BF16)
    rows = k_refs[0].shape[1]
    pv = jnp.dot(pb[:, 0:rows], v_refs[0][0].astype(BF16), preferred_element_type=F32)
    for i in range(1, n_pg):
        pv = pv + jnp.dot(pb[:, i * rows:(i + 1) * rows], v_refs[i][0].astype(BF16), preferred_element_type=F32)
    acc_ref[...] = alpha * acc_ref[...] + pv
    m_ref[...] = m_new

    @pl.when(ci == pl.num_programs(1) - 1)
    def _():
        o = acc_ref[...] / l_ref[...]
        lam = _diff_lambda(lam_ref, lam_init)
        for h in range(nh):
            w = o[2 * h:2 * h + 1] - lam * o[2 * h + 1:2 * h + 2]
            o_ref[0, :, h * LANES:(h + 1) * LANES] = _sub_norm(w, g_ref[...], lam_init).astype(BF16)


def _attn_decode(q, k_new, v_new, cache_k, cache_v, page_table, layer, lam_vecs, g_sub, lam_init, n_pg):
    db, aw = q.shape
    _, n_pool, page, nh, hd = cache_k.shape
    n_pages = page_table.shape[1]
    ck = cache_k.reshape(-1, page * nh, hd)
    cv = cache_v.reshape(-1, page * nh, hd)
    pt = page_table.reshape(-1) + layer * n_pool
    fixed = lambda b, c, pt_ref: (0, 0)
    per_seq = lambda b, c, pt_ref: (b, 0, 0)

    def page_spec(i):
        return pl.BlockSpec((1, page * nh, hd), lambda b, c, pt_ref: (pt_ref[b * n_pages + c * n_pg + i], 0, 0))

    grid_spec = pltpu.PrefetchScalarGridSpec(
        num_scalar_prefetch=1, grid=(db, n_pages // n_pg),
        in_specs=[pl.BlockSpec(lam_vecs.shape, fixed), pl.BlockSpec(g_sub.shape, fixed),
                  pl.BlockSpec((1, nh, hd), per_seq), pl.BlockSpec((1, nh, hd), per_seq),
                  pl.BlockSpec((1, nh, hd), per_seq)]
        + [page_spec(i) for i in range(n_pg)] * 2,
        out_specs=pl.BlockSpec((1, 1, aw), per_seq),
        scratch_shapes=[pltpu.VMEM((2 * nh, 1), F32), pltpu.VMEM((2 * nh, 1), F32), pltpu.VMEM((2 * nh, hd), F32)])
    out = pl.pallas_call(
        functools.partial(_attn_decode_kernel, lam_init=lam_init, n_pg=n_pg),
        grid_spec=grid_spec, out_shape=jax.ShapeDtypeStruct((db, 1, aw), BF16),
        compiler_params=_params(("parallel", "arbitrary")), name="attn_decode",
    )(pt, lam_vecs, g_sub, q.reshape(db, nh, hd), k_new.reshape(db, nh, hd), v_new.reshape(db, nh, hd),
      *([ck] * n_pg), *([cv] * n_pg))
    return out.reshape(db, aw)


def _pool_sample_kernel(st_ref, u_ref, w_pool_ref, scale_ref, zp_ref, new_ref, *, pos):
    pw = u_ref.shape[1]
    u = u_ref[...]
    for g, w in enumerate(POOL_WINDOWS):
        cols = slice(g * LANES, (g + 1) * LANES)
        acc = u[:, cols]
        for k in range(1, w):
            r = POOL_CTX - k
            acc = acc + st_ref[:, r * pw + g * LANES:r * pw + (g + 1) * LANES]
        z = acc / float(min(w, pos + 1)) - u[:, cols]
        y = jnp.dot(z.astype(BF16), w_pool_ref[g], preferred_element_type=F32)
        zp_ref[:, cols] = (y * scale_ref[:, cols]).astype(BF16)
    new_ref[:, 0:(POOL_CTX - 1) * pw] = st_ref[:, pw:POOL_CTX * pw]
    new_ref[:, (POOL_CTX - 1) * pw:] = u


def _pool_sample(state, u, w_pool, scale, pos):
    db, ctx, pw = state.shape
    st2 = state.reshape(db, ctx * pw)
    zp, new = pl.pallas_call(
        functools.partial(_pool_sample_kernel, pos=pos),
        out_shape=[jax.ShapeDtypeStruct((db, pw), BF16), jax.ShapeDtypeStruct((db, ctx * pw), F32)],
        compiler_params=pltpu.CompilerParams(vmem_limit_bytes=VMEM_LIMIT), name="pool_sample",
    )(st2, u, w_pool, scale)
    return zp, new.reshape(db, ctx, pw)


def _route(lg):
    lane = lax.broadcasted_iota(jnp.int32, lg.shape, 1)
    big = jnp.int32(LANES)
    is_g = lane < N_GROUPS
    mg = jnp.max(jnp.where(is_g, lg, NEG_INF), axis=1, keepdims=True)
    g_sel = jnp.min(jnp.where(is_g & (lg == mg), lane, big), axis=1, keepdims=True)
    g_w = 1.0 / jnp.sum(jnp.where(is_g, jnp.exp(lg - mg), 0.0), axis=1, keepdims=True)
    lo = EXPERT_LANE0 + g_sel * N_PER_GROUP
    in_grp = (lane >= lo) & (lane < lo + N_PER_GROUP)
    v1 = jnp.max(jnp.where(in_grp, lg, NEG_INF), axis=1, keepdims=True)
    i1 = jnp.min(jnp.where(in_grp & (lg == v1), lane, big), axis=1, keepdims=True)
    rest = in_grp & (lane != i1)
    v2 = jnp.max(jnp.where(rest, lg, NEG_INF), axis=1, keepdims=True)
    i2 = jnp.min(jnp.where(rest & (lg == v2), lane, big), axis=1, keepdims=True)
    e2 = jnp.exp(v2 - v1)
    w1 = g_w / (1.0 + e2)
    w2 = g_w * e2 / (1.0 + e2)
    return jnp.where(lane == i1, w1, 0.0) + jnp.where(lane == i2, w2, 0.0)


def _mixout_kernel(x_ref, o_ref, zp_ref, wo_ref, g_ref, wr_ref, h_ref, n_ref, comb_ref):
    aw = o_ref.shape[1]
    mix = jnp.dot(o_ref[...], wo_ref[0:aw, :], preferred_element_type=F32)
    mix = mix + jnp.dot(zp_ref[...], wo_ref[aw:, :], preferred_element_type=F32)
    h = x_ref[...] + mix
    h_ref[...] = h
    n = _rms(h, g_ref[...])
    nb = n.astype(BF16)
    n_ref[...] = nb
    tm = nb.shape[0]
    n_lo = (n - nb.astype(F32)).astype(BF16)
    r = jnp.dot(jnp.concatenate([nb, n_lo], axis=0), wr_ref[...], preferred_element_type=F32)
    lg = (r[:tm, :LANES] + r[:tm, LANES:]) + (r[tm:, :LANES] + r[tm:, LANES:])
    comb_ref[...] = _route(lg)


def _mixout(x, o, zp, w_out, g_ffn, w_router, tm):
    n_tok, d = x.shape
    aw, pw = o.shape[1], zp.shape[1]
    row = lambda i: (i, 0)
    fixed = lambda i: (0, 0)
    return pl.pallas_call(
        _mixout_kernel, grid=(n_tok // tm,),
        in_specs=[pl.BlockSpec((tm, d), row), pl.BlockSpec((tm, aw), row), pl.BlockSpec((tm, pw), row),
                  pl.BlockSpec(w_out.shape, fixed), pl.BlockSpec((1, d), fixed),
                  pl.BlockSpec(w_router.shape, fixed)],
        out_specs=[pl.BlockSpec((tm, d), row), pl.BlockSpec((tm, d), row), pl.BlockSpec((tm, LANES), row)],
        out_shape=[jax.ShapeDtypeStruct((n_tok, d), F32), jax.ShapeDtypeStruct((n_tok, d), BF16),
                   jax.ShapeDtypeStruct((n_tok, LANES), F32)],
        compiler_params=_params(("parallel",)), name="mixout",
    )(x, o, zp, w_out, g_ffn, w_router)


def _experts_kernel(n_ref, comb_ref, h_ref, wg_ref, wu_ref, wd_ref, out_ref, acc_ref):
    e = pl.program_id(1)

    @pl.when(e == 0)
    def _():
        acc_ref[...] = h_ref[...]

    x = n_ref[...]
    comb = comb_ref[...]
    lane = lax.broadcasted_iota(jnp.int32, comb.shape, 1)
    c_e = jnp.sum(jnp.where(lane == EXPERT_LANE0 + e, comb, 0.0), axis=1, keepdims=True)
    gate = jnp.dot(x, wg_ref[0], preferred_element_type=F32)
    up = jnp.dot(x, wu_ref[0], preferred_element_type=F32)
    hid = gate * jax.nn.sigmoid(gate) * up * c_e
    acc_ref[...] += jnp.dot(hid.astype(BF16), wd_ref[0], preferred_element_type=F32)

    @pl.when(e == pl.num_programs(1) - 1)
    def _():
        out_ref[...] = acc_ref[...]


def _experts(n, comb, h, w_g, w_u, w_d, tm):
    n_tok, d = h.shape
    ne, _, ff = w_g.shape
    row = lambda i, e: (i, 0)
    return pl.pallas_call(
        _experts_kernel, grid=(n_tok // tm, ne),
        in_specs=[pl.BlockSpec((tm, d), row), pl.BlockSpec((tm, LANES), row), pl.BlockSpec((tm, d), row),
                  pl.BlockSpec((1, d, ff), lambda i, e: (e, 0, 0)), pl.BlockSpec((1, d, ff), lambda i, e: (e, 0, 0)),
                  pl.BlockSpec((1, ff, d), lambda i, e: (e, 0, 0))],
        out_specs=pl.BlockSpec((tm, d), row), out_shape=jax.ShapeDtypeStruct((n_tok, d), F32),
        scratch_shapes=[pltpu.VMEM((tm, d), F32)],
        compiler_params=_params(("parallel", "arbitrary")), name="experts",
    )(n, comb, h, w_g, w_u, w_d)


def _ple_kernel(h_ref, p_ref, g_ref, wpg_ref, wpp_ref, gf_ref, out_ref, *, final):
    h = h_ref[...]
    n = _rms(h, g_ref[...])
    gate = jax.nn.sigmoid(jnp.dot(n.astype(BF16), wpg_ref[...], preferred_element_type=F32))
    emb = jnp.dot(p_ref[...].astype(BF16), wpp_ref[...], preferred_element_type=F32)
    h = h + gate * emb
    out_ref[...] = _rms(h, gf_ref[...]) if final else h


def _ple(h, p, g_ple, w_pg, w_pp, g_final, final, tm):
    n_tok, d = h.shape
    pd = p.shape[1]
    row = lambda i: (i, 0)
    fixed = lambda i: (0, 0)
    return pl.pallas_call(
        functools.partial(_ple_kernel, final=final), grid=(n_tok // tm,),
        in_specs=[pl.BlockSpec((tm, d), row), pl.BlockSpec((tm, pd), row), pl.BlockSpec((1, d), fixed),
                  pl.BlockSpec((d, d), fixed), pl.BlockSpec((pd, d), fixed), pl.BlockSpec((1, d), fixed)],
        out_specs=pl.BlockSpec((tm, d), row), out_shape=jax.ShapeDtypeStruct((n_tok, d), F32),
        compiler_params=_params(("parallel",)), name="ple",
    )(h, p, g_ple, w_pg, w_pp, g_final)


def _token_tile(n_tok, want):
    tm = min(want, n_tok)
    assert n_tok % tm == 0, (n_tok, tm)
    return tm


def kernel(x_prompt, x_sample, cache_k, cache_v, state_pool, page_table, p_prompt, p_sample, g_mix_norm, w_in, lambda_q1, lambda_k1, lambda_q2, lambda_k2, g_subln, w_pool, pool_scale, w_out, g_ffn_norm, w_router_group, w_router_expert, w_gate_e, w_up_e, w_down_e, g_ple_norm, w_ple_gate, w_ple_proj, g_final):
    b, s, d = x_prompt.shape
    db, t_new, _ = x_sample.shape
    assert t_new == 1, "the decode kernel handles one new token per sequence"
    depth = w_in.shape[0]
    page = cache_k.shape[2]
    nh, hd = cache_k.shape[3], cache_k.shape[4]
    past_len = page_table.shape[1] * page
    pw = pool_scale.shape[1]
    aw = nh * hd
    ff = w_gate_e.shape[-1]

    tm_p = _token_tile(s, 512)
    tm_s = _token_tile(db, 128)
    tq = _token_tile(s, 512)
    tm_e = _token_tile(b * s, 1024)
    n_pg = 16
    assert page_table.shape[1] % n_pg == 0

    tab_p = _rope_tables(jnp.arange(s))
    tab_s = _rope_tables(jnp.full((tm_s,), past_len))

    h_p = x_prompt.reshape(b * s, d)
    h_s = x_sample.reshape(db, d)
    outs = [[] for _ in range(6)]
    for l in range(depth):
        lam_init = 0.8 - 0.6 * math.exp(-0.3 * l)
        lam_vecs = jnp.stack([lambda_q1[l], lambda_k1[l], lambda_q2[l], lambda_k2[l]])
        g_sub = g_subln[l][None]
        w_in_l = w_in[l].astype(BF16)
        w_pool_l = w_pool[l].astype(BF16)
        scale_l = pool_scale[l][None]
        w_out_l = w_out[l].astype(BF16)
        w_r = jnp.concatenate([w_router_group[l], w_router_expert[l].reshape(d, N_EXPERTS)], axis=1)
        w_r = jnp.pad(w_r, ((0, 0), (0, LANES - w_r.shape[1])))
        w_r_hi = w_r.astype(BF16)
        w_r = jnp.concatenate([w_r_hi, (w_r - w_r_hi.astype(F32)).astype(BF16)], axis=1)
        w_g = w_gate_e[l].reshape(N_EXPERTS, d, ff).astype(BF16)
        w_u = w_up_e[l].reshape(N_EXPERTS, d, ff).astype(BF16)
        w_d = w_down_e[l].reshape(N_EXPERTS, ff, d).astype(BF16)
        w_pg = w_ple_gate[l].astype(BF16)
        w_pp = w_ple_proj[l].astype(BF16)
        final = l == depth - 1

        def channel(h, o, zp, p, tm_mix, tm_exp, tm_ple):
            h1, n2, comb = _mixout(h, o, zp, w_out_l, g_ffn_norm[l][None], w_r, tm_mix)
            h2 = _experts(n2, comb, h1, w_g, w_u, w_d, tm_exp)
            return _ple(h2, p, g_ple_norm[l][None], w_pg, w_pp, g_final[None], final, tm_ple)

        q, k, v, kb, vb, zp, tails = _inproj(h_p, g_mix_norm[l][None], w_in_l, tab_p, aw, tm_p, s // tm_p,
                                             pool_args=(w_pool_l, scale_l))
        o = _attn_prompt(q.reshape(b, s, aw), kb.reshape(b, s, aw), vb.reshape(b, s, aw),
                         lam_vecs, g_sub, lam_init, tq)
        h_p = channel(h_p, o.reshape(b * s, aw), zp, p_prompt[l].reshape(b * s, -1), tm_p, tm_e, tm_p)
        outs[0].append(k.reshape(b, s, nh, hd))
        outs[1].append(v.reshape(b, s, nh, hd))
        outs[2].append(tails.reshape(b, s // tm_p, HALO, pw)[:, -1, HALO - POOL_CTX:])

        q, k, v, u = _inproj(h_s, g_mix_norm[l][None], w_in_l, tab_s, aw, tm_s, 1)
        o = _attn_decode(q, k, v, cache_k, cache_v, page_table, l, lam_vecs, g_sub, lam_init, n_pg)
        zp, pool_new = _pool_sample(state_pool[l], u, w_pool_l, scale_l, past_len)
        h_s = channel(h_s, o, zp, p_sample[l].reshape(db, -1), tm_s, tm_s, tm_s)
        outs[3].append(k.reshape(db, 1, nh, hd))
        outs[4].append(v.reshape(db, 1, nh, hd))
        outs[5].append(pool_new)

    y_prompt = h_p.reshape(b, s, d)
    y_sample = h_s.reshape(db, 1, d)
    return (y_prompt, y_sample, *[jnp.stack(o, axis=0) for o in outs])
```

```python
import functools
import math

import jax
import jax.numpy as jnp
from jax import lax
from jax.experimental import pallas as pl
from jax.experimental.pallas import tpu as pltpu

F32 = jnp.float32
BF16 = jnp.bfloat16

LANES = 128
SUBLANES = 8
DIFF_HEAD_DIM = 64
ROT_DIM = DIFF_HEAD_DIM // 4
ROT_HALF = ROT_DIM // 2
ROPE_THETA = 500000.0
POOL_WINDOWS = (2, 4, 8, 16)
POOL_CTX = max(POOL_WINDOWS) - 1
HALO = 16
RMS_EPS = 1e-6
N_GROUPS = 4
N_PER_GROUP = 4
N_EXPERTS = N_GROUPS * N_PER_GROUP
EXPERT_LANE0 = N_GROUPS
NEG_INF = float("-inf")
ATTN_HEADS_PER_STEP = 2
MOE_CHUNK = 288
MOE_ROW_BLOCKS = (512, 384, 256, 128)
VMEM_LIMIT = 56 * 1024 * 1024


def _rms(x, g):
    return x * lax.rsqrt(jnp.mean(x * x, axis=-1, keepdims=True) + RMS_EPS) * g


def _params(sem):
    return pltpu.CompilerParams(dimension_semantics=sem, vmem_limit_bytes=VMEM_LIMIT)


def _rope_tables(pos):
    inv = ROPE_THETA ** (-jnp.arange(0, ROT_DIM, 2, dtype=F32) / ROT_DIM)
    ang = pos.astype(F32)[:, None] * inv[None, :]
    cos, sin = jnp.cos(ang), jnp.sin(ang)
    lane = jnp.arange(LANES)
    r = lane % DIFF_HEAD_DIM
    i = r % ROT_HALF
    c = jnp.where(r < ROT_DIM, cos[:, i], 1.0)
    sa = jnp.where(r < ROT_HALF, -sin[:, i], 0.0)
    sb = jnp.where((r >= ROT_HALF) & (r < ROT_DIM), sin[:, i], 0.0)
    return c, sa, sb


def _pool_mix(ue_ref, u, pos0, w_pool_ref, scale_ref, zp_ref):
    tm = u.shape[0]
    pos = pos0 + lax.broadcasted_iota(jnp.int32, (tm, 1), 0)
    for g, w in enumerate(POOL_WINDOWS):
        cols = slice(g * LANES, (g + 1) * LANES)
        acc = u[:, cols]
        for k in range(1, w):
            acc = acc + ue_ref[HALO - k:HALO - k + tm, cols]
        cnt = jnp.minimum(w, pos + 1).astype(F32)
        z = acc / cnt - u[:, cols]
        y = jnp.dot(z.astype(BF16), w_pool_ref[g], preferred_element_type=F32)
        zp_ref[:, cols] = (y * scale_ref[:, cols]).astype(BF16)


def _inproj_kernel(x_ref, g_ref, w_ref, c_ref, sa_ref, sb_ref, *rest, pool, tiles_per_seq):
    if pool:
        w_pool_ref, scale_ref, q_ref, k_ref, v_ref, kb_ref, vb_ref, zp_ref, tail_ref, ue_ref = rest
    else:
        q_ref, k_ref, v_ref, u_ref = rest
    tm = x_ref.shape[0]
    aw = q_ref.shape[1]
    nh = aw // LANES
    n = _rms(x_ref[...], g_ref[...])
    z = jnp.dot(n.astype(BF16), w_ref[...], preferred_element_type=F32)
    c, sa, sb = c_ref[...], sa_ref[...], sb_ref[...]

    def rope(blk):
        return blk * c + pltpu.roll(blk, LANES - ROT_HALF, 1) * sa + pltpu.roll(blk, ROT_HALF, 1) * sb

    q_scale = DIFF_HEAD_DIM ** -0.5
    for j in range(nh):
        cols = slice(j * LANES, (j + 1) * LANES)
        q_ref[:, cols] = (rope(z[:, j * LANES:(j + 1) * LANES]) * q_scale).astype(BF16)
        k = rope(z[:, aw + j * LANES:aw + (j + 1) * LANES])
        v = z[:, 2 * aw + j * LANES:2 * aw + (j + 1) * LANES]
        k_ref[pl.ds(j, tm, stride=nh), :] = k
        v_ref[pl.ds(j, tm, stride=nh), :] = v
        if pool:
            kb_ref[:, cols] = k.astype(BF16)
            vb_ref[:, cols] = v.astype(BF16)
    u = z[:, 3 * aw:]
    if not pool:
        u_ref[...] = u
        return
    t = pl.program_id(0) % tiles_per_seq

    @pl.when(t == 0)
    def _():
        ue_ref[0:HALO, :] = jnp.zeros((HALO, u.shape[1]), F32)

    @pl.when(t != 0)
    def _():
        ue_ref[0:HALO, :] = ue_ref[tm:tm + HALO, :]

    ue_ref[HALO:HALO + tm, :] = u
    tail_ref[0] = u[tm - HALO:, :]
    _pool_mix(ue_ref, u, t * tm, w_pool_ref, scale_ref, zp_ref)


def _inproj(x, g, w_in, tables, aw, tm, tiles_per_seq, pool_args=None):
    n_tok, d = x.shape
    in_w = w_in.shape[1]
    pw = in_w - 3 * aw
    nh = aw // LANES
    pool = pool_args is not None
    grid = (n_tok // tm,)
    row = lambda i: (i, 0)
    fixed = lambda i: (0, 0)
    tab = lambda i: (i % tiles_per_seq, 0)
    in_specs = [pl.BlockSpec((tm, d), row), pl.BlockSpec((1, d), fixed), pl.BlockSpec((d, in_w), fixed),
                pl.BlockSpec((tm, LANES), tab), pl.BlockSpec((tm, LANES), tab), pl.BlockSpec((tm, LANES), tab)]
    args = [x, g, w_in, *tables]
    out_shape = [jax.ShapeDtypeStruct((n_tok, aw), BF16), jax.ShapeDtypeStruct((n_tok * nh, LANES), F32),
                 jax.ShapeDtypeStruct((n_tok * nh, LANES), F32)]
    out_specs = [pl.BlockSpec((tm, aw), row), pl.BlockSpec((tm * nh, LANES), row), pl.BlockSpec((tm * nh, LANES), row)]
    scratch = []
    if pool:
        w_pool, scale = pool_args
        in_specs += [pl.BlockSpec(w_pool.shape, lambda i: (0, 0, 0)), pl.BlockSpec((1, pw), fixed)]
        args += [w_pool, scale]
        out_shape += [jax.ShapeDtypeStruct((n_tok, aw), BF16), jax.ShapeDtypeStruct((n_tok, aw), BF16),
                      jax.ShapeDtypeStruct((n_tok, pw), BF16), jax.ShapeDtypeStruct((grid[0], HALO, pw), F32)]
        out_specs += [pl.BlockSpec((tm, aw), row), pl.BlockSpec((tm, aw), row),
                      pl.BlockSpec((tm, pw), row), pl.BlockSpec((1, HALO, pw), lambda i: (i, 0, 0))]
        scratch = [pltpu.VMEM((tm + HALO, pw), F32)]
    else:
        out_shape += [jax.ShapeDtypeStruct((n_tok, pw), F32)]
        out_specs += [pl.BlockSpec((tm, pw), row)]
    return pl.pallas_call(
        functools.partial(_inproj_kernel, pool=pool, tiles_per_seq=tiles_per_seq),
        grid=grid, in_specs=in_specs, out_specs=out_specs, out_shape=out_shape, scratch_shapes=scratch,
        compiler_params=_params(("arbitrary",)), name="inproj_pool" if pool else "inproj",
    )(*args)


def _diff_lambda(lam_ref, lam_init):
    lv = lam_ref[...]
    a = jnp.exp(jnp.sum(lv[0:1] * lv[1:2], axis=-1, keepdims=True))
    b = jnp.exp(jnp.sum(lv[2:3] * lv[3:4], axis=-1, keepdims=True))
    return a - b + lam_init


def _sub_norm(w, g, lam_init):
    return _rms(w, g) * (1.0 - lam_init)


def _stack_maps(q):
    lane = lax.broadcasted_iota(jnp.int32, q.shape, 1)
    first = jnp.where(lane < DIFF_HEAD_DIM, 1.0, 0.0).astype(q.dtype)
    return jnp.concatenate([q * first, q * (1.0 - first)], axis=0)


def _attn_prompt_kernel(lam_ref, g_ref, q_ref, k_ref, v_ref, o_ref, kb_ref, vb_ref, m_ref, acc_ref, s_ref,
                        *, lam_init):
    tq = o_ref.shape[1]
    nhs = o_ref.shape[2] // LANES
    qi = pl.program_id(2)
    nq = pl.num_programs(2)

    def stacked_q(blk):
        rows = pl.ds(pl.multiple_of(blk * tq, tq), tq)
        return [_stack_maps(q_ref[0, rows, h * LANES:(h + 1) * LANES]) for h in range(nhs)]

    def scores_of(qs, h, j):
        start = pl.multiple_of(j * tq, tq)
        return lax.dot_general(qs[h], kb_ref[h, pl.ds(start, tq), :], (((1,), (1,)), ((), ())),
                               preferred_element_type=F32)

    qq = stacked_q(qi)
    scores = functools.partial(scores_of, qq)

    @pl.when(qi == 0)
    def _():
        for h in range(nhs):
            cols = slice(h * LANES, (h + 1) * LANES)
            kb_ref[h] = k_ref[0, :, cols]
            vb_ref[h, :, 0:LANES] = v_ref[0, :, cols]
            vb_ref[h, :, LANES:] = jnp.ones((vb_ref.shape[1], LANES), BF16)
        for h in range(nhs):
            s_ref[h] = scores(h, 0)

    m_ref[...] = jnp.full(m_ref.shape, NEG_INF, F32)
    acc_ref[...] = jnp.zeros(acc_ref.shape, F32)

    def update(h, j, masked):
        start = pl.multiple_of(j * tq, tq)

        def load_scores():
            s = s_ref[h]
            if masked:
                r = lax.broadcasted_iota(jnp.int32, s.shape, 0)
                r = jnp.where(r >= tq, r - tq, r)
                c = lax.broadcasted_iota(jnp.int32, s.shape, 1)
                s = jnp.where(c <= r, s, NEG_INF)
            return s

        m_prev = m_ref[h]
        m_new = jnp.maximum(m_prev, jnp.max(load_scores(), axis=1, keepdims=True))
        alpha = jnp.exp(m_prev - m_new)
        p = jnp.exp(load_scores() - jnp.concatenate([m_new] * (tq // LANES), axis=1))
        pv = jnp.dot(p.astype(BF16), vb_ref[h, pl.ds(start, tq), :], preferred_element_type=F32)
        acc_ref[h] = jnp.concatenate([alpha, alpha], axis=1) * acc_ref[h] + pv
        m_ref[h] = m_new

    def body(j, carry):
        for h in range(nhs):
            update(h, j, masked=False)
            s_ref[h] = scores(h, j + 1)
        return carry

    lax.fori_loop(0, qi, body, 0)
    lam = _diff_lambda(lam_ref, lam_init)
    q_next = stacked_q(jnp.minimum(qi + 1, nq - 1))
    for h in range(nhs):
        update(h, qi, masked=True)
        s_ref[h] = scores_of(q_next, h, 0)
        acc = acc_ref[h]
        o = acc[:, 0:LANES] / acc[:, LANES:]
        w = o[:tq] - lam * o[tq:]
        o_ref[0, :, h * LANES:(h + 1) * LANES] = _sub_norm(w, g_ref[...], lam_init).astype(BF16)


def _attn_prompt(q, k, v, lam_vecs, g_sub, lam_init, tq):
    b, s, aw = q.shape
    nhs = ATTN_HEADS_PER_STEP
    hw = nhs * LANES
    grid = (b, aw // hw, s // tq)
    fixed = lambda bi, h, qi: (0, 0)
    return pl.pallas_call(
        functools.partial(_attn_prompt_kernel, lam_init=lam_init),
        grid=grid,
        in_specs=[pl.BlockSpec(lam_vecs.shape, fixed), pl.BlockSpec(g_sub.shape, fixed),
                  pl.BlockSpec((1, s, hw), lambda bi, h, qi: (bi, 0, h)),
                  pl.BlockSpec((1, s, hw), lambda bi, h, qi: (bi, 0, h)),
                  pl.BlockSpec((1, s, hw), lambda bi, h, qi: (bi, 0, h))],
        out_specs=pl.BlockSpec((1, tq, hw), lambda bi, h, qi: (bi, qi, h)),
        out_shape=jax.ShapeDtypeStruct((b, s, aw), BF16),
        scratch_shapes=[pltpu.VMEM((nhs, s, LANES), BF16), pltpu.VMEM((nhs, s, 2 * LANES), BF16),
                        pltpu.VMEM((nhs, 2 * tq, LANES), F32), pltpu.VMEM((nhs, 2 * tq, 2 * LANES), F32),
                        pltpu.VMEM((nhs, 2 * tq, tq), F32)],
        compiler_params=_params(("parallel", "parallel", "arbitrary")), name="attn_prompt",
    )(lam_vecs, g_sub, q, k, v)


def _attn_decode_kernel(pt_ref, lam_ref, g_ref, q_ref, kn_ref, vn_ref, *rest, lam_init, n_pg):
    k_refs, v_refs = rest[:n_pg], rest[n_pg:2 * n_pg]
    o_ref, m_ref, l_ref, acc_ref = rest[2 * n_pg:]
    ci = pl.program_id(1)
    nh = q_ref.shape[1]
    nmap = 2 * nh

    def per_map(x):
        return jnp.concatenate([x[h:h + 1] for h in range(nh) for _ in range(2)], axis=0)

    row = lax.broadcasted_iota(jnp.int32, (nmap, LANES), 0)
    lane = lax.broadcasted_iota(jnp.int32, (nmap, LANES), 1)
    qm_f32 = jnp.where(lane // DIFF_HEAD_DIM == row % 2, per_map(q_ref[0].astype(F32)), 0.0)
    qm = qm_f32.astype(BF16)

    @pl.when(ci == 0)
    def _():
        m_ref[...] = jnp.sum(qm_f32 * per_map(kn_ref[0]), axis=1, keepdims=True)
        l_ref[...] = jnp.ones(l_ref.shape, F32)
        acc_ref[...] = per_map(vn_ref[0])

    s = jnp.concatenate(
        [lax.dot_general(qm, kr[0].astype(BF16), (((1,), (1,)), ((), ())), preferred_element_type=F32)
         for kr in k_refs], axis=1)
    srow = lax.broadcasted_iota(jnp.int32, s.shape, 0)
    scol = lax.broadcasted_iota(jnp.int32, s.shape, 1)
    s = jnp.where(scol % nh == srow // 2, s, NEG_INF)
    m_prev = m_ref[...]
    m_new = jnp.maximum(m_prev, jnp.max(s, axis=1, keepdims=True))
    alpha = jnp.exp(m_prev - m_new)
    p = jnp.exp(s - m_new)
    l_ref[...] = alpha * l_ref[...] + jnp.sum(p, axis=1, keepdims=True)
    pb = p.astype(BF16)
    rows = k_refs[0].shape[1]
    pv = jnp.dot(pb[:, 0:rows], v_refs[0][0].astype(BF16), preferred_element_type=F32)
    for i in range(1, n_pg):
        pv = pv + jnp.dot(pb[:, i * rows:(i + 1) * rows], v_refs[i][0].astype(BF16), preferred_element_type=F32)
    acc_ref[...] = alpha * acc_ref[...] + pv
    m_ref[...] = m_new

    @pl.when(ci == pl.num_programs(1) - 1)
    def _():
        o = acc_ref[...] / l_ref[...]
        lam = _diff_lambda(lam_ref, lam_init)
        for h in range(nh):
            w = o[2 * h:2 * h + 1] - lam * o[2 * h + 1:2 * h + 2]
            o_ref[0, :, h * LANES:(h + 1) * LANES] = _sub_norm(w, g_ref[...], lam_init).astype(BF16)


def _attn_decode(q, k_new, v_new, cache_k, cache_v, page_table, layer, lam_vecs, g_sub, lam_init, n_pg):
    db, aw = q.shape
    _, n_pool, page, nh, hd = cache_k.shape
    n_pages = page_table.shape[1]
    ck = cache_k.reshape(-1, page * nh, hd)
    cv = cache_v.reshape(-1, page * nh, hd)
    pt = page_table.reshape(-1) + layer * n_pool
    fixed = lambda b, c, pt_ref: (0, 0)
    per_seq = lambda b, c, pt_ref: (b, 0, 0)

    def page_spec(i):
        return pl.BlockSpec((1, page * nh, hd), lambda b, c, pt_ref: (pt_ref[b * n_pages + c * n_pg + i], 0, 0))

    grid_spec = pltpu.PrefetchScalarGridSpec(
        num_scalar_prefetch=1, grid=(db, n_pages // n_pg),
        in_specs=[pl.BlockSpec(lam_vecs.shape, fixed), pl.BlockSpec(g_sub.shape, fixed),
                  pl.BlockSpec((1, nh, hd), per_seq), pl.BlockSpec((1, nh, hd), per_seq),
                  pl.BlockSpec((1, nh, hd), per_seq)]
        + [page_spec(i) for i in range(n_pg)] * 2,
        out_specs=pl.BlockSpec((1, 1, aw), per_seq),
        scratch_shapes=[pltpu.VMEM((2 * nh, 1), F32), pltpu.VMEM((2 * nh, 1), F32), pltpu.VMEM((2 * nh, hd), F32)])
    out = pl.pallas_call(
        functools.partial(_attn_decode_kernel, lam_init=lam_init, n_pg=n_pg),
        grid_spec=grid_spec, out_shape=jax.ShapeDtypeStruct((db, 1, aw), BF16),
        compiler_params=_params(("parallel", "arbitrary")), name="attn_decode",
    )(pt, lam_vecs, g_sub, q.reshape(db, nh, hd), k_new.reshape(db, nh, hd), v_new.reshape(db, nh, hd),
      *([ck] * n_pg), *([cv] * n_pg))
    return out.reshape(db, aw)


def _pool_sample_kernel(st_ref, u_ref, w_pool_ref, scale_ref, zp_ref, new_ref, *, pos):
    pw = u_ref.shape[1]
    u = u_ref[...]
    for g, w in enumerate(POOL_WINDOWS):
        cols = slice(g * LANES, (g + 1) * LANES)
        acc = u[:, cols]
        for k in range(1, w):
            r = POOL_CTX - k
            acc = acc + st_ref[:, r * pw + g * LANES:r * pw + (g + 1) * LANES]
        z = acc / float(min(w, pos + 1)) - u[:, cols]
        y = jnp.dot(z.astype(BF16), w_pool_ref[g], preferred_element_type=F32)
        zp_ref[:, cols] = (y * scale_ref[:, cols]).astype(BF16)
    new_ref[:, 0:(POOL_CTX - 1) * pw] = st_ref[:, pw:POOL_CTX * pw]
    new_ref[:, (POOL_CTX - 1) * pw:] = u


def _pool_sample(state, u, w_pool, scale, pos):
    db, ctx, pw = state.shape
    st2 = state.reshape(db, ctx * pw)
    zp, new = pl.pallas_call(
        functools.partial(_pool_sample_kernel, pos=pos),
        out_shape=[jax.ShapeDtypeStruct((db, pw), BF16), jax.ShapeDtypeStruct((db, ctx * pw), F32)],
        compiler_params=pltpu.CompilerParams(vmem_limit_bytes=VMEM_LIMIT), name="pool_sample",
    )(st2, u, w_pool, scale)
    return zp, new.reshape(db, ctx, pw)


def _route(lg):
    lane = lax.broadcasted_iota(jnp.int32, lg.shape, 1)
    big = jnp.int32(LANES)
    is_g = lane < N_GROUPS
    mg = jnp.max(jnp.where(is_g, lg, NEG_INF), axis=1, keepdims=True)
    g_sel = jnp.min(jnp.where(is_g & (lg == mg), lane, big), axis=1, keepdims=True)
    g_w = 1.0 / jnp.sum(jnp.where(is_g, jnp.exp(lg - mg), 0.0), axis=1, keepdims=True)
    lo = EXPERT_LANE0 + g_sel * N_PER_GROUP
    in_grp = (lane >= lo) & (lane < lo + N_PER_GROUP)
    v1 = jnp.max(jnp.where(in_grp, lg, NEG_INF), axis=1, keepdims=True)
    i1 = jnp.min(jnp.where(in_grp & (lg == v1), lane, big), axis=1, keepdims=True)
    rest = in_grp & (lane != i1)
    v2 = jnp.max(jnp.where(rest, lg, NEG_INF), axis=1, keepdims=True)
    i2 = jnp.min(jnp.where(rest & (lg == v2), lane, big), axis=1, keepdims=True)
    e2 = jnp.exp(v2 - v1)
    w1 = g_w / (1.0 + e2)
    w2 = g_w * e2 / (1.0 + e2)
    comb = jnp.where(lane == i1, w1, 0.0) + jnp.where(lane == i2, w2, 0.0)
    return jnp.where(lane == 0, g_sel.astype(F32), comb), g_sel


def _mixout_kernel(x_ref, o_ref, zp_ref, wo_ref, g_ref, wr_ref, h_ref, n_ref, comb_ref, cnt_ref):
    aw = o_ref.shape[1]
    mix = jnp.dot(o_ref[...], wo_ref[0:aw, :], preferred_element_type=F32)
    mix = mix + jnp.dot(zp_ref[...], wo_ref[aw:, :], preferred_element_type=F32)
    h = x_ref[...] + mix
    h_ref[...] = h
    n = _rms(h, g_ref[...])
    nb = n.astype(BF16)
    n_ref[...] = nb
    tm = nb.shape[0]
    n_lo = (n - nb.astype(F32)).astype(BF16)
    r = jnp.dot(jnp.concatenate([nb, n_lo], axis=0), wr_ref[...], preferred_element_type=F32)
    lg = (r[:tm, :LANES] + r[:tm, LANES:]) + (r[tm:, :LANES] + r[tm:, LANES:])
    comb, g_sel = _route(lg)
    comb_ref[...] = comb
    lane = lax.broadcasted_iota(jnp.int32, comb.shape, 1)
    cnt = jnp.sum(jnp.where(lane == g_sel, 1.0, 0.0), axis=0, keepdims=True)
    cnt_ref[0] = jnp.broadcast_to(cnt, cnt_ref.shape[1:]).astype(jnp.int32)


def _mixout(x, o, zp, w_out, g_ffn, w_router, tm):
    n_tok, d = x.shape
    aw, pw = o.shape[1], zp.shape[1]
    row = lambda i: (i, 0)
    fixed = lambda i: (0, 0)
    return pl.pallas_call(
        _mixout_kernel, grid=(n_tok // tm,),
        in_specs=[pl.BlockSpec((tm, d), row), pl.BlockSpec((tm, aw), row), pl.BlockSpec((tm, pw), row),
                  pl.BlockSpec(w_out.shape, fixed), pl.BlockSpec((1, d), fixed),
                  pl.BlockSpec(w_router.shape, fixed)],
        out_specs=[pl.BlockSpec((tm, d), row), pl.BlockSpec((tm, d), row), pl.BlockSpec((tm, LANES), row),
                   pl.BlockSpec((1, SUBLANES, LANES), lambda i: (i, 0, 0))],
        out_shape=[jax.ShapeDtypeStruct((n_tok, d), F32), jax.ShapeDtypeStruct((n_tok, d), BF16),
                   jax.ShapeDtypeStruct((n_tok, LANES), F32),
                   jax.ShapeDtypeStruct((n_tok // tm, SUBLANES, LANES), jnp.int32)],
        compiler_params=_params(("parallel",)), name="mixout",
    )(x, o, zp, w_out, g_ffn, w_router)


SEG_ALIGN = 16


def _moe_kernel(cnt_ref, n_ref, comb_ref, wg_ref, wu_ref, wd_ref, out_ref, tri_ref, xs_ref, ys_ref, cw_ref, pos_ref,
                *, ch, rb, n_map):
    i, g = pl.program_id(0), pl.program_id(1)
    t, d = n_ref.shape
    r_rows = xs_ref.shape[0]
    counts = [cnt_ref[i * N_GROUPS + k] for k in range(N_GROUPS)]
    offs = [jnp.int32(0)]
    for k in range(N_GROUPS - 1):
        offs.append(offs[-1] + (counts[k] + SEG_ALIGN - 1) // SEG_ALIGN * SEG_ALIGN)

    @pl.when((i == 0) & (g == 0))
    def _():
        r = lax.broadcasted_iota(jnp.int32, (t, t), 0)
        c = lax.broadcasted_iota(jnp.int32, (t, t), 1)
        tri_ref[...] = jnp.where(r < c, 1.0, 0.0).astype(BF16)
        xs_ref[n_map:, :] = jnp.zeros((r_rows - n_map, d), BF16)
        cw_ref[n_map:, :] = jnp.zeros((r_rows - n_map, LANES), F32)

    @pl.when(g == 0)
    def _():
        comb = comb_ref[...]
        g_lm = comb.T[0:1, :]
        sub = lax.broadcasted_iota(jnp.int32, (SUBLANES, t), 0)
        memb = jnp.where(g_lm == sub.astype(F32), 1.0, 0.0)
        rank = jnp.dot(memb.astype(BF16), tri_ref[...], preferred_element_type=F32)
        off_b = jnp.zeros((SUBLANES, t), F32)
        for k in range(1, N_GROUPS):
            off_b = jnp.where(sub == k, offs[k].astype(F32), off_b)
        pos = jnp.sum(memb * (rank + off_b), axis=0, keepdims=True)
        pos_ref[...] = jnp.broadcast_to(pos, (LANES, t)).T
        comb_hi = comb.astype(BF16)
        comb_hl = jnp.concatenate([comb_hi, (comb - comb_hi.astype(F32)).astype(BF16)], axis=1)
        x = n_ref[...]
        for r0 in range(0, n_map, rb):
            rows = r0 + lax.broadcasted_iota(jnp.int32, (rb, t), 0)
            onehot = jnp.where(rows.astype(F32) == pos, 1.0, 0.0).astype(BF16)
            xs_ref[r0:r0 + rb, :] = jnp.dot(onehot, x, preferred_element_type=F32).astype(BF16)
            cw = jnp.dot(onehot, comb_hl, preferred_element_type=F32)
            cw_ref[r0:r0 + rb, :] = cw[:, :LANES] + cw[:, LANES:]
        ys_ref[0:n_map, :] = jnp.zeros((n_map, d), BF16)

    off_g, cnt_g = offs[0], counts[0]
    for k in range(1, N_GROUPS):
        off_g = jnp.where(g == k, offs[k], off_g)
        cnt_g = jnp.where(g == k, counts[k], cnt_g)
    lane = lax.broadcasted_iota(jnp.int32, (ch, LANES), 1)

    def chunk(c, carry):
        rows = pl.ds(pl.multiple_of(off_g + c * ch, SEG_ALIGN), ch)
        x = xs_ref[rows, :]
        cw = cw_ref[rows, :]
        y = jnp.zeros((ch, d), F32)
        for e in range(N_PER_GROUP):
            c_e = jnp.sum(jnp.where(lane == EXPERT_LANE0 + g * N_PER_GROUP + e, cw, 0.0), axis=1, keepdims=True)
            gate = jnp.dot(x, wg_ref[e], preferred_element_type=F32)
            up = jnp.dot(x, wu_ref[e], preferred_element_type=F32)
            hid = gate * jax.nn.sigmoid(gate) * up * c_e
            y = y + jnp.dot(hid.astype(BF16), wd_ref[e], preferred_element_type=F32)
        ys_ref[rows, :] = y.astype(BF16)
        return carry

    lax.fori_loop(0, (cnt_g + ch - 1) // ch, chunk, 0)

    @pl.when(g == pl.num_programs(1) - 1)
    def _():
        pos_t = jnp.concatenate([pos_ref[...]] * (n_map // LANES), axis=1)
        cols = lax.broadcasted_iota(jnp.int32, (t, n_map), 1)
        onehot = jnp.where(cols.astype(F32) == pos_t, 1.0, 0.0).astype(BF16)
        out_ref[...] = jnp.dot(onehot, ys_ref[0:n_map, :], preferred_element_type=F32)


def _moe(n, comb, counts, w_g, w_u, w_d, tm):
    n_tok, d = n.shape
    ff = w_g.shape[2]
    ch = min(MOE_CHUNK, tm)
    n_map = -(-(tm + N_GROUPS * SEG_ALIGN) // LANES) * LANES
    r_rows = n_map + ch
    rb = next(r for r in MOE_ROW_BLOCKS if n_map % r == 0)
    row = lambda i, g, cnt: (i, 0)
    grp = lambda i, g, cnt: (g, 0, 0)
    grid_spec = pltpu.PrefetchScalarGridSpec(
        num_scalar_prefetch=1, grid=(n_tok // tm, N_GROUPS),
        in_specs=[pl.BlockSpec((tm, d), row), pl.BlockSpec((tm, LANES), row),
                  pl.BlockSpec((N_PER_GROUP, d, ff), grp), pl.BlockSpec((N_PER_GROUP, d, ff), grp),
                  pl.BlockSpec((N_PER_GROUP, ff, d), grp)],
        out_specs=pl.BlockSpec((tm, d), row),
        scratch_shapes=[pltpu.VMEM((tm, tm), BF16), pltpu.VMEM((r_rows, d), BF16), pltpu.VMEM((r_rows, d), BF16),
                        pltpu.VMEM((r_rows, LANES), F32), pltpu.VMEM((tm, LANES), F32)])
    return pl.pallas_call(
        functools.partial(_moe_kernel, ch=ch, rb=rb, n_map=n_map), grid_spec=grid_spec,
        out_shape=jax.ShapeDtypeStruct((n_tok, d), F32),
        compiler_params=_params(("arbitrary", "arbitrary")), name="experts",
    )(counts, n, comb, w_g, w_u, w_d)


def _ple_kernel(h_ref, moe_ref, p_ref, g_ref, wpg_ref, wpp_ref, gf_ref, out_ref, *, final):
    h = h_ref[...] + moe_ref[...]
    n = _rms(h, g_ref[...])
    gate = jax.nn.sigmoid(jnp.dot(n.astype(BF16), wpg_ref[...], preferred_element_type=F32))
    emb = jnp.dot(p_ref[...].astype(BF16), wpp_ref[...], preferred_element_type=F32)
    h = h + gate * emb
    out_ref[...] = _rms(h, gf_ref[...]) if final else h


def _ple(h, moe, p, g_ple, w_pg, w_pp, g_final, final, tm):
    n_tok, d = h.shape
    pd = p.shape[1]
    row = lambda i: (i, 0)
    fixed = lambda i: (0, 0)
    return pl.pallas_call(
        functools.partial(_ple_kernel, final=final), grid=(n_tok // tm,),
        in_specs=[pl.BlockSpec((tm, d), row), pl.BlockSpec((tm, d), row), pl.BlockSpec((tm, pd), row),
                  pl.BlockSpec((1, d), fixed), pl.BlockSpec((d, d), fixed), pl.BlockSpec((pd, d), fixed),
                  pl.BlockSpec((1, d), fixed)],
        out_specs=pl.BlockSpec((tm, d), row), out_shape=jax.ShapeDtypeStruct((n_tok, d), F32),
        compiler_params=_params(("parallel",)), name="ple",
    )(h, moe, p, g_ple, w_pg, w_pp, g_final)


def _token_tile(n_tok, want):
    tm = min(want, n_tok)
    assert n_tok % tm == 0, (n_tok, tm)
    return tm


def kernel(x_prompt, x_sample, cache_k, cache_v, state_pool, page_table, p_prompt, p_sample, g_mix_norm, w_in, lambda_q1, lambda_k1, lambda_q2, lambda_k2, g_subln, w_pool, pool_scale, w_out, g_ffn_norm, w_router_group, w_router_expert, w_gate_e, w_up_e, w_down_e, g_ple_norm, w_ple_gate, w_ple_proj, g_final):
    b, s, d = x_prompt.shape
    db, t_new, _ = x_sample.shape
    assert t_new == 1, "the decode kernel handles one new token per sequence"
    depth = w_in.shape[0]
    page = cache_k.shape[2]
    nh, hd = cache_k.shape[3], cache_k.shape[4]
    past_len = page_table.shape[1] * page
    pw = pool_scale.shape[1]
    aw = nh * hd
    ff = w_gate_e.shape[-1]

    tm_p = _token_tile(s, 512)
    tm_s = _token_tile(db, 128)
    tq = _token_tile(s, 512)
    tm_e = _token_tile(b * s, 1024)
    n_pg = 16
    assert page_table.shape[1] % n_pg == 0

    tab_p = _rope_tables(jnp.arange(s))
    tab_s = _rope_tables(jnp.full((tm_s,), past_len))

    h_p = x_prompt.reshape(b * s, d)
    h_s = x_sample.reshape(db, d)
    outs = [[] for _ in range(6)]
    for l in range(depth):
        lam_init = 0.8 - 0.6 * math.exp(-0.3 * l)
        lam_vecs = jnp.stack([lambda_q1[l], lambda_k1[l], lambda_q2[l], lambda_k2[l]])
        g_sub = g_subln[l][None]
        w_in_l = w_in[l].astype(BF16)
        w_pool_l = w_pool[l].astype(BF16)
        scale_l = pool_scale[l][None]
        w_out_l = w_out[l].astype(BF16)
        w_r = jnp.concatenate([w_router_group[l], w_router_expert[l].reshape(d, N_EXPERTS)], axis=1)
        w_r = jnp.pad(w_r, ((0, 0), (0, LANES - w_r.shape[1])))
        w_r_hi = w_r.astype(BF16)
        w_r = jnp.concatenate([w_r_hi, (w_r - w_r_hi.astype(F32)).astype(BF16)], axis=1)
        w_g = w_gate_e[l].reshape(N_EXPERTS, d, ff).astype(BF16)
        w_u = w_up_e[l].reshape(N_EXPERTS, d, ff).astype(BF16)
        w_d = w_down_e[l].reshape(N_EXPERTS, ff, d).astype(BF16)
        w_pg = w_ple_gate[l].astype(BF16)
        w_pp = w_ple_proj[l].astype(BF16)
        final = l == depth - 1

        def channel(h, o, zp, p, tm_mix, tm_exp, tm_ple):
            h1, n2, comb, cnt = _mixout(h, o, zp, w_out_l, g_ffn_norm[l][None], w_r, tm_mix)
            cnt = cnt[:, 0, :N_GROUPS].reshape(-1, tm_exp // tm_mix, N_GROUPS).sum(axis=1).reshape(-1)
            moe = _moe(n2, comb, cnt, w_g, w_u, w_d, tm_exp)
            return _ple(h1, moe, p, g_ple_norm[l][None], w_pg, w_pp, g_final[None], final, tm_ple)

        q, k, v, kb, vb, zp, tails = _inproj(h_p, g_mix_norm[l][None], w_in_l, tab_p, aw, tm_p, s // tm_p,
                                             pool_args=(w_pool_l, scale_l))
        o = _attn_prompt(q.reshape(b, s, aw), kb.reshape(b, s, aw), vb.reshape(b, s, aw),
                         lam_vecs, g_sub, lam_init, tq)
        h_p = channel(h_p, o.reshape(b * s, aw), zp, p_prompt[l].reshape(b * s, -1), tm_p, tm_e, tm_p)
        outs[0].append(k.reshape(b, s, nh, hd))
        outs[1].append(v.reshape(b, s, nh, hd))
        outs[2].append(tails.reshape(b, s // tm_p, HALO, pw)[:, -1, HALO - POOL_CTX:])

        q, k, v, u = _inproj(h_s, g_mix_norm[l][None], w_in_l, tab_s, aw, tm_s, 1)
        o = _attn_decode(q, k, v, cache_k, cache_v, page_table, l, lam_vecs, g_sub, lam_init, n_pg)
        zp, pool_new = _pool_sample(state_pool[l], u, w_pool_l, scale_l, past_len)
        h_s = channel(h_s, o, zp, p_sample[l].reshape(db, -1), tm_s, tm_s, tm_s)
        outs[3].append(k.reshape(db, 1, nh, hd))
        outs[4].append(v.reshape(db, 1, nh, hd))
        outs[5].append(pool_new)

    y_prompt = h_p.reshape(b, s, d)
    y_sample = h_s.reshape(db, 1, d)
    return (y_prompt, y_sample, *[jnp.stack(o, axis=0) for o in outs])
```

```python
import functools
import math

import jax
import jax.numpy as jnp
from jax import lax
from jax.experimental import pallas as pl
from jax.experimental.pallas import tpu as pltpu

F32 = jnp.float32
BF16 = jnp.bfloat16

LANES = 128
SUBLANES = 8
DIFF_HEAD_DIM = 64
ROT_DIM = DIFF_HEAD_DIM // 4
ROT_HALF = ROT_DIM // 2
ROPE_THETA = 500000.0
POOL_WINDOWS = (2, 4, 8, 16)
POOL_CTX = max(POOL_WINDOWS) - 1
HALO = 16
RMS_EPS = 1e-6
N_GROUPS = 4
N_PER_GROUP = 4
N_EXPERTS = N_GROUPS * N_PER_GROUP
EXPERT_LANE0 = N_GROUPS
NEG_INF = float("-inf")
ATTN_HEADS_PER_STEP = 2
MOE_CHUNK = 288
MOE_ROW_BLOCKS = (512, 384, 256, 128)
VMEM_LIMIT = 56 * 1024 * 1024


def _rms(x, g):
    return x * lax.rsqrt(jnp.mean(x * x, axis=-1, keepdims=True) + RMS_EPS) * g


def _params(sem):
    return pltpu.CompilerParams(dimension_semantics=sem, vmem_limit_bytes=VMEM_LIMIT)


def _rope_tables(pos):
    inv = ROPE_THETA ** (-jnp.arange(0, ROT_DIM, 2, dtype=F32) / ROT_DIM)
    ang = pos.astype(F32)[:, None] * inv[None, :]
    cos, sin = jnp.cos(ang), jnp.sin(ang)
    lane = jnp.arange(LANES)
    r = lane % DIFF_HEAD_DIM
    i = r % ROT_HALF
    c = jnp.where(r < ROT_DIM, cos[:, i], 1.0)
    sa = jnp.where(r < ROT_HALF, -sin[:, i], 0.0)
    sb = jnp.where((r >= ROT_HALF) & (r < ROT_DIM), sin[:, i], 0.0)
    return c, sa, sb


def _pool_mix(ue_ref, u, pos0, w_pool_ref, scale_ref, zp_ref):
    tm = u.shape[0]
    pos = pos0 + lax.broadcasted_iota(jnp.int32, (tm, 1), 0)
    for g, w in enumerate(POOL_WINDOWS):
        cols = slice(g * LANES, (g + 1) * LANES)
        acc = u[:, cols]
        for k in range(1, w):
            acc = acc + ue_ref[HALO - k:HALO - k + tm, cols]
        cnt = jnp.minimum(w, pos + 1).astype(F32)
        z = acc / cnt - u[:, cols]
        y = jnp.dot(z.astype(BF16), w_pool_ref[g], preferred_element_type=F32)
        zp_ref[:, cols] = (y * scale_ref[:, cols]).astype(BF16)


def _inproj_kernel(x_ref, g_ref, w_ref, c_ref, sa_ref, sb_ref, *rest, pool, tiles_per_seq):
    if pool:
        w_pool_ref, scale_ref, q_ref, k_ref, v_ref, kb_ref, vb_ref, zp_ref, tail_ref, ue_ref = rest
    else:
        q_ref, k_ref, v_ref, u_ref = rest
    tm = x_ref.shape[0]
    aw = q_ref.shape[1]
    nh = aw // LANES
    n = _rms(x_ref[...], g_ref[...])
    z = jnp.dot(n.astype(BF16), w_ref[...], preferred_element_type=F32)
    c, sa, sb = c_ref[...], sa_ref[...], sb_ref[...]

    def rope(blk):
        return blk * c + pltpu.roll(blk, LANES - ROT_HALF, 1) * sa + pltpu.roll(blk, ROT_HALF, 1) * sb

    q_scale = DIFF_HEAD_DIM ** -0.5
    for j in range(nh):
        cols = slice(j * LANES, (j + 1) * LANES)
        q_ref[:, cols] = (rope(z[:, j * LANES:(j + 1) * LANES]) * q_scale).astype(BF16)
        k = rope(z[:, aw + j * LANES:aw + (j + 1) * LANES])
        v = z[:, 2 * aw + j * LANES:2 * aw + (j + 1) * LANES]
        k_ref[pl.ds(j, tm, stride=nh), :] = k
        v_ref[pl.ds(j, tm, stride=nh), :] = v
        if pool:
            kb_ref[:, cols] = k.astype(BF16)
            vb_ref[:, cols] = v.astype(BF16)
    u = z[:, 3 * aw:]
    if not pool:
        u_ref[...] = u
        return
    t = pl.program_id(0) % tiles_per_seq

    @pl.when(t == 0)
    def _():
        ue_ref[0:HALO, :] = jnp.zeros((HALO, u.shape[1]), F32)

    @pl.when(t != 0)
    def _():
        ue_ref[0:HALO, :] = ue_ref[tm:tm + HALO, :]

    ue_ref[HALO:HALO + tm, :] = u
    tail_ref[0] = u[tm - HALO:, :]
    _pool_mix(ue_ref, u, t * tm, w_pool_ref, scale_ref, zp_ref)


def _inproj(x, g, w_in, tables, aw, tm, tiles_per_seq, pool_args=None):
    n_tok, d = x.shape
    in_w = w_in.shape[1]
    pw = in_w - 3 * aw
    nh = aw // LANES
    pool = pool_args is not None
    grid = (n_tok // tm,)
    row = lambda i: (i, 0)
    fixed = lambda i: (0, 0)
    tab = lambda i: (i % tiles_per_seq, 0)
    in_specs = [pl.BlockSpec((tm, d), row), pl.BlockSpec((1, d), fixed), pl.BlockSpec((d, in_w), fixed),
                pl.BlockSpec((tm, LANES), tab), pl.BlockSpec((tm, LANES), tab), pl.BlockSpec((tm, LANES), tab)]
    args = [x, g, w_in, *tables]
    out_shape = [jax.ShapeDtypeStruct((n_tok, aw), BF16), jax.ShapeDtypeStruct((n_tok * nh, LANES), F32),
                 jax.ShapeDtypeStruct((n_tok * nh, LANES), F32)]
    out_specs = [pl.BlockSpec((tm, aw), row), pl.BlockSpec((tm * nh, LANES), row), pl.BlockSpec((tm * nh, LANES), row)]
    scratch = []
    if pool:
        w_pool, scale = pool_args
        in_specs += [pl.BlockSpec(w_pool.shape, lambda i: (0, 0, 0)), pl.BlockSpec((1, pw), fixed)]
        args += [w_pool, scale]
        out_shape += [jax.ShapeDtypeStruct((n_tok, aw), BF16), jax.ShapeDtypeStruct((n_tok, aw), BF16),
                      jax.ShapeDtypeStruct((n_tok, pw), BF16), jax.ShapeDtypeStruct((grid[0], HALO, pw), F32)]
        out_specs += [pl.BlockSpec((tm, aw), row), pl.BlockSpec((tm, aw), row),
                      pl.BlockSpec((tm, pw), row), pl.BlockSpec((1, HALO, pw), lambda i: (i, 0, 0))]
        scratch = [pltpu.VMEM((tm + HALO, pw), F32)]
    else:
        out_shape += [jax.ShapeDtypeStruct((n_tok, pw), F32)]
        out_specs += [pl.BlockSpec((tm, pw), row)]
    return pl.pallas_call(
        functools.partial(_inproj_kernel, pool=pool, tiles_per_seq=tiles_per_seq),
        grid=grid, in_specs=in_specs, out_specs=out_specs, out_shape=out_shape, scratch_shapes=scratch,
        compiler_params=_params(("arbitrary",)), name="inproj_pool" if pool else "inproj",
    )(*args)


def _diff_lambda(lam_ref, lam_init):
    lv = lam_ref[...]
    a = jnp.exp(jnp.sum(lv[0:1] * lv[1:2], axis=-1, keepdims=True))
    b = jnp.exp(jnp.sum(lv[2:3] * lv[3:4], axis=-1, keepdims=True))
    return a - b + lam_init


def _sub_norm(w, g, lam_init):
    return _rms(w, g) * (1.0 - lam_init)


def _stack_maps(q):
    lane = lax.broadcasted_iota(jnp.int32, q.shape, 1)
    first = jnp.where(lane < DIFF_HEAD_DIM, 1.0, 0.0).astype(q.dtype)
    return jnp.concatenate([q * first, q * (1.0 - first)], axis=0)


def _attn_prompt_kernel(lam_ref, g_ref, q_ref, k_ref, v_ref, o_ref, kb_ref, vb_ref, m_ref, acc_ref, s_ref,
                        *, lam_init):
    tq = o_ref.shape[1]
    nhs = o_ref.shape[2] // LANES
    qi = pl.program_id(2)
    nq = pl.num_programs(2)

    def stacked_q(blk):
        rows = pl.ds(pl.multiple_of(blk * tq, tq), tq)
        return [_stack_maps(q_ref[0, rows, h * LANES:(h + 1) * LANES]) for h in range(nhs)]

    def scores_of(qs, h, j):
        start = pl.multiple_of(j * tq, tq)
        return lax.dot_general(qs[h], kb_ref[h, pl.ds(start, tq), :], (((1,), (1,)), ((), ())),
                               preferred_element_type=F32)

    qq = stacked_q(qi)
    scores = functools.partial(scores_of, qq)

    @pl.when(qi == 0)
    def _():
        for h in range(nhs):
            cols = slice(h * LANES, (h + 1) * LANES)
            kb_ref[h] = k_ref[0, :, cols]
            vb_ref[h, :, 0:LANES] = v_ref[0, :, cols]
            vb_ref[h, :, LANES:] = jnp.ones((vb_ref.shape[1], LANES), BF16)
        for h in range(nhs):
            s_ref[h] = scores(h, 0)

    m_ref[...] = jnp.full(m_ref.shape, NEG_INF, F32)
    acc_ref[...] = jnp.zeros(acc_ref.shape, F32)

    def update(h, j, masked):
        start = pl.multiple_of(j * tq, tq)

        def load_scores():
            s = s_ref[h]
            if masked:
                r = lax.broadcasted_iota(jnp.int32, s.shape, 0)
                r = jnp.where(r >= tq, r - tq, r)
                c = lax.broadcasted_iota(jnp.int32, s.shape, 1)
                s = jnp.where(c <= r, s, NEG_INF)
            return s

        m_prev = m_ref[h]
        m_new = jnp.maximum(m_prev, jnp.max(load_scores(), axis=1, keepdims=True))
        alpha = jnp.exp(m_prev - m_new)
        p = jnp.exp(load_scores() - jnp.concatenate([m_new] * (tq // LANES), axis=1))
        pv = jnp.dot(p.astype(BF16), vb_ref[h, pl.ds(start, tq), :], preferred_element_type=F32)
        acc_ref[h] = jnp.concatenate([alpha, alpha], axis=1) * acc_ref[h] + pv
        m_ref[h] = m_new

    def body(j, carry):
        for h in range(nhs):
            update(h, j, masked=False)
            s_ref[h] = scores(h, j + 1)
        return carry

    lax.fori_loop(0, qi, body, 0)
    lam = _diff_lambda(lam_ref, lam_init)
    q_next = stacked_q(jnp.minimum(qi + 1, nq - 1))
    for h in range(nhs):
        update(h, qi, masked=True)
        s_ref[h] = scores_of(q_next, h, 0)
        acc = acc_ref[h]
        o = acc[:, 0:LANES] / acc[:, LANES:]
        w = o[:tq] - lam * o[tq:]
        o_ref[0, :, h * LANES:(h + 1) * LANES] = _sub_norm(w, g_ref[...], lam_init).astype(BF16)


def _attn_prompt(q, k, v, lam_vecs, g_sub, lam_init, tq):
    b, s, aw = q.shape
    nhs = ATTN_HEADS_PER_STEP
    hw = nhs * LANES
    grid = (b, aw // hw, s // tq)
    fixed = lambda bi, h, qi: (0, 0)
    return pl.pallas_call(
        functools.partial(_attn_prompt_kernel, lam_init=lam_init),
        grid=grid,
        in_specs=[pl.BlockSpec(lam_vecs.shape, fixed), pl.BlockSpec(g_sub.shape, fixed),
                  pl.BlockSpec((1, s, hw), lambda bi, h, qi: (bi, 0, h)),
                  pl.BlockSpec((1, s, hw), lambda bi, h, qi: (bi, 0, h)),
                  pl.BlockSpec((1, s, hw), lambda bi, h, qi: (bi, 0, h))],
        out_specs=pl.BlockSpec((1, tq, hw), lambda bi, h, qi: (bi, qi, h)),
        out_shape=jax.ShapeDtypeStruct((b, s, aw), BF16),
        scratch_shapes=[pltpu.VMEM((nhs, s, LANES), BF16), pltpu.VMEM((nhs, s, 2 * LANES), BF16),
                        pltpu.VMEM((nhs, 2 * tq, LANES), F32), pltpu.VMEM((nhs, 2 * tq, 2 * LANES), F32),
                        pltpu.VMEM((nhs, 2 * tq, tq), F32)],
        compiler_params=_params(("parallel", "parallel", "arbitrary")), name="attn_prompt",
    )(lam_vecs, g_sub, q, k, v)


def _attn_decode_kernel(pt_ref, lam_ref, g_ref, q_ref, kn_ref, vn_ref, *rest, lam_init, n_pg):
    k_refs, v_refs = rest[:n_pg], rest[n_pg:2 * n_pg]
    o_ref, m_ref, l_ref, acc_ref = rest[2 * n_pg:]
    ci = pl.program_id(1)
    nh = q_ref.shape[1]
    nmap = 2 * nh

    def per_map(x):
        return jnp.concatenate([x[h:h + 1] for h in range(nh) for _ in range(2)], axis=0)

    row = lax.broadcasted_iota(jnp.int32, (nmap, LANES), 0)
    lane = lax.broadcasted_iota(jnp.int32, (nmap, LANES), 1)
    qm_f32 = jnp.where(lane // DIFF_HEAD_DIM == row % 2, per_map(q_ref[0].astype(F32)), 0.0)
    qm = qm_f32.astype(BF16)

    @pl.when(ci == 0)
    def _():
        m_ref[...] = jnp.sum(qm_f32 * per_map(kn_ref[0]), axis=1, keepdims=True)
        l_ref[...] = jnp.ones(l_ref.shape, F32)
        acc_ref[...] = per_map(vn_ref[0])

    s = jnp.concatenate(
        [lax.dot_general(qm, kr[0].astype(BF16), (((1,), (1,)), ((), ())), preferred_element_type=F32)
         for kr in k_refs], axis=1)
    srow = lax.broadcasted_iota(jnp.int32, s.shape, 0)
    scol = lax.broadcasted_iota(jnp.int32, s.shape, 1)
    s = jnp.where(scol % nh == srow // 2, s, NEG_INF)
    m_prev = m_ref[...]
    m_new = jnp.maximum(m_prev, jnp.max(s, axis=1, keepdims=True))
    alpha = jnp.exp(m_prev - m_new)
    p = jnp.exp(s - m_new)
    l_ref[...] = alpha * l_ref[...] + jnp.sum(p, axis=1, keepdims=True)
    pb = p.astype(BF16)
    rows = k_refs[0].shape[1]
    pv = jnp.dot(pb[:, 0:rows], v_refs[0][0].astype(BF16), preferred_element_type=F32)
    for i in range(1, n_pg):
        pv = pv + jnp.dot(pb[:, i * rows:(i + 1) * rows], v_refs[i][0].astype(BF16), preferred_element_type=F32)
    acc_ref[...] = alpha * acc_ref[...] + pv
    m_ref[...] = m_new

    @pl.when(ci == pl.num_programs(1) - 1)
    def _():
        o = acc_ref[...] / l_ref[...]
        lam = _diff_lambda(lam_ref, lam_init)
        for h in range(nh):
            w = o[2 * h:2 * h + 1] - lam * o[2 * h + 1:2 * h + 2]
            o_ref[0, :, h * LANES:(h + 1) * LANES] = _sub_norm(w, g_ref[...], lam_init).astype(BF16)


def _attn_decode(q, k_new, v_new, cache_k, cache_v, page_table, layer, lam_vecs, g_sub, lam_init, n_pg):
    db, aw = q.shape
    _, n_pool, page, nh, hd = cache_k.shape
    n_pages = page_table.shape[1]
    ck = cache_k.reshape(-1, page * nh, hd)
    cv = cache_v.reshape(-1, page * nh, hd)
    pt = page_table.reshape(-1) + layer * n_pool
    fixed = lambda b, c, pt_ref: (0, 0)
    per_seq = lambda b, c, pt_ref: (b, 0, 0)

    def page_spec(i):
        return pl.BlockSpec((1, page * nh, hd), lambda b, c, pt_ref: (pt_ref[b * n_pages + c * n_pg + i], 0, 0))

    grid_spec = pltpu.PrefetchScalarGridSpec(
        num_scalar_prefetch=1, grid=(db, n_pages // n_pg),
        in_specs=[pl.BlockSpec(lam_vecs.shape, fixed), pl.BlockSpec(g_sub.shape, fixed),
                  pl.BlockSpec((1, nh, hd), per_seq), pl.BlockSpec((1, nh, hd), per_seq),
                  pl.BlockSpec((1, nh, hd), per_seq)]
        + [page_spec(i) for i in range(n_pg)] * 2,
        out_specs=pl.BlockSpec((1, 1, aw), per_seq),
        scratch_shapes=[pltpu.VMEM((2 * nh, 1), F32), pltpu.VMEM((2 * nh, 1), F32), pltpu.VMEM((2 * nh, hd), F32)])
    out = pl.pallas_call(
        functools.partial(_attn_decode_kernel, lam_init=lam_init, n_pg=n_pg),
        grid_spec=grid_spec, out_shape=jax.ShapeDtypeStruct((db, 1, aw), BF16),
        compiler_params=_params(("parallel", "arbitrary")), name="attn_decode",
    )(pt, lam_vecs, g_sub, q.reshape(db, nh, hd), k_new.reshape(db, nh, hd), v_new.reshape(db, nh, hd),
      *([ck] * n_pg), *([cv] * n_pg))
    return out.reshape(db, aw)


def _pool_sample_kernel(st_ref, u_ref, w_pool_ref, scale_ref, zp_ref, new_ref, *, pos):
    pw = u_ref.shape[1]
    u = u_ref[...]
    for g, w in enumerate(POOL_WINDOWS):
        cols = slice(g * LANES, (g + 1) * LANES)
        acc = u[:, cols]
        for k in range(1, w):
            r = POOL_CTX - k
            acc = acc + st_ref[:, r * pw + g * LANES:r * pw + (g + 1) * LANES]
        z = acc / float(min(w, pos + 1)) - u[:, cols]
        y = jnp.dot(z.astype(BF16), w_pool_ref[g], preferred_element_type=F32)
        zp_ref[:, cols] = (y * scale_ref[:, cols]).astype(BF16)
    new_ref[:, 0:(POOL_CTX - 1) * pw] = st_ref[:, pw:POOL_CTX * pw]
    new_ref[:, (POOL_CTX - 1) * pw:] = u


def _pool_sample(state, u, w_pool, scale, pos):
    db, ctx, pw = state.shape
    st2 = state.reshape(db, ctx * pw)
    zp, new = pl.pallas_call(
        functools.partial(_pool_sample_kernel, pos=pos),
        out_shape=[jax.ShapeDtypeStruct((db, pw), BF16), jax.ShapeDtypeStruct((db, ctx * pw), F32)],
        compiler_params=pltpu.CompilerParams(vmem_limit_bytes=VMEM_LIMIT), name="pool_sample",
    )(st2, u, w_pool, scale)
    return zp, new.reshape(db, ctx, pw)


def _route(lg):
    lane = lax.broadcasted_iota(jnp.int32, lg.shape, 1)
    big = jnp.int32(LANES)
    is_g = lane < N_GROUPS
    mg = jnp.max(jnp.where(is_g, lg, NEG_INF), axis=1, keepdims=True)
    g_sel = jnp.min(jnp.where(is_g & (lg == mg), lane, big), axis=1, keepdims=True)
    g_w = 1.0 / jnp.sum(jnp.where(is_g, jnp.exp(lg - mg), 0.0), axis=1, keepdims=True)
    lo = EXPERT_LANE0 + g_sel * N_PER_GROUP
    in_grp = (lane >= lo) & (lane < lo + N_PER_GROUP)
    v1 = jnp.max(jnp.where(in_grp, lg, NEG_INF), axis=1, keepdims=True)
    i1 = jnp.min(jnp.where(in_grp & (lg == v1), lane, big), axis=1, keepdims=True)
    rest = in_grp & (lane != i1)
    v2 = jnp.max(jnp.where(rest, lg, NEG_INF), axis=1, keepdims=True)
    i2 = jnp.min(jnp.where(rest & (lg == v2), lane, big), axis=1, keepdims=True)
    e2 = jnp.exp(v2 - v1)
    w1 = g_w / (1.0 + e2)
    w2 = g_w * e2 / (1.0 + e2)
    comb = jnp.where(lane == i1, w1, 0.0) + jnp.where(lane == i2, w2, 0.0)
    return jnp.where(lane == 0, g_sel.astype(F32), comb), g_sel


def _mixout_kernel(x_ref, o_ref, zp_ref, wo_ref, g_ref, wr_ref, h_ref, n_ref, comb_ref, cnt_ref):
    aw = o_ref.shape[1]
    mix = jnp.dot(o_ref[...], wo_ref[0:aw, :], preferred_element_type=F32)
    mix = mix + jnp.dot(zp_ref[...], wo_ref[aw:, :], preferred_element_type=F32)
    h = x_ref[...] + mix
    h_ref[...] = h
    n = _rms(h, g_ref[...])
    nb = n.astype(BF16)
    n_ref[...] = nb
    tm = nb.shape[0]
    n_lo = (n - nb.astype(F32)).astype(BF16)
    r = jnp.dot(jnp.concatenate([nb, n_lo], axis=0), wr_ref[...], preferred_element_type=F32)
    lg = (r[:tm, :LANES] + r[:tm, LANES:]) + (r[tm:, :LANES] + r[tm:, LANES:])
    comb, g_sel = _route(lg)
    comb_ref[...] = comb
    lane = lax.broadcasted_iota(jnp.int32, comb.shape, 1)
    cnt = jnp.sum(jnp.where(lane == g_sel, 1.0, 0.0), axis=0, keepdims=True)
    cnt_ref[0] = jnp.broadcast_to(cnt, cnt_ref.shape[1:]).astype(jnp.int32)


def _mixout(x, o, zp, w_out, g_ffn, w_router, tm):
    n_tok, d = x.shape
    aw, pw = o.shape[1], zp.shape[1]
    row = lambda i: (i, 0)
    fixed = lambda i: (0, 0)
    return pl.pallas_call(
        _mixout_kernel, grid=(n_tok // tm,),
        in_specs=[pl.BlockSpec((tm, d), row), pl.BlockSpec((tm, aw), row), pl.BlockSpec((tm, pw), row),
                  pl.BlockSpec(w_out.shape, fixed), pl.BlockSpec((1, d), fixed),
                  pl.BlockSpec(w_router.shape, fixed)],
        out_specs=[pl.BlockSpec((tm, d), row), pl.BlockSpec((tm, d), row), pl.BlockSpec((tm, LANES), row),
                   pl.BlockSpec((1, SUBLANES, LANES), lambda i: (i, 0, 0))],
        out_shape=[jax.ShapeDtypeStruct((n_tok, d), F32), jax.ShapeDtypeStruct((n_tok, d), BF16),
                   jax.ShapeDtypeStruct((n_tok, LANES), F32),
                   jax.ShapeDtypeStruct((n_tok // tm, SUBLANES, LANES), jnp.int32)],
        compiler_params=_params(("parallel",)), name="mixout",
    )(x, o, zp, w_out, g_ffn, w_router)


SEG_ALIGN = 16


def _moe_kernel(cnt_ref, n_ref, comb_ref, wg_ref, wu_ref, wd_ref, out_ref, tri_ref, xs_ref, ys_ref, cw_ref, pos_ref,
                *, ch, rb, n_map):
    i, g = pl.program_id(0), pl.program_id(1)
    t, d = n_ref.shape
    r_rows = xs_ref.shape[0]
    counts = [cnt_ref[i * N_GROUPS + k] for k in range(N_GROUPS)]
    offs = [jnp.int32(0)]
    for k in range(N_GROUPS - 1):
        offs.append(offs[-1] + (counts[k] + SEG_ALIGN - 1) // SEG_ALIGN * SEG_ALIGN)

    @pl.when((i == 0) & (g == 0))
    def _():
        r = lax.broadcasted_iota(jnp.int32, (t, t), 0)
        c = lax.broadcasted_iota(jnp.int32, (t, t), 1)
        tri_ref[...] = jnp.where(r < c, 1.0, 0.0).astype(BF16)
        xs_ref[n_map:, :] = jnp.zeros((r_rows - n_map, d), BF16)
        cw_ref[n_map:, :] = jnp.zeros((r_rows - n_map, LANES), F32)

    @pl.when(g == 0)
    def _():
        comb = comb_ref[...]
        g_lm = comb.T[0:1, :]
        sub = lax.broadcasted_iota(jnp.int32, (SUBLANES, t), 0)
        memb = jnp.where(g_lm == sub.astype(F32), 1.0, 0.0)
        rank = jnp.dot(memb.astype(BF16), tri_ref[...], preferred_element_type=F32)
        off_b = jnp.zeros((SUBLANES, t), F32)
        for k in range(1, N_GROUPS):
            off_b = jnp.where(sub == k, offs[k].astype(F32), off_b)
        pos = jnp.sum(memb * (rank + off_b), axis=0, keepdims=True)
        pos_ref[...] = jnp.broadcast_to(pos, (LANES, t)).T
        comb_hi = comb.astype(BF16)
        comb_hl = jnp.concatenate([comb_hi, (comb - comb_hi.astype(F32)).astype(BF16)], axis=1)
        x = n_ref[...]
        for r0 in range(0, n_map, rb):
            rows = r0 + lax.broadcasted_iota(jnp.int32, (rb, t), 0)
            onehot = jnp.where(rows.astype(F32) == pos, 1.0, 0.0).astype(BF16)
            xs_ref[r0:r0 + rb, :] = jnp.dot(onehot, x, preferred_element_type=F32).astype(BF16)
            cw = jnp.dot(onehot, comb_hl, preferred_element_type=F32)
            cw_ref[r0:r0 + rb, :] = cw[:, :LANES] + cw[:, LANES:]
        ys_ref[0:n_map, :] = jnp.zeros((n_map, d), BF16)

    off_g, cnt_g = offs[0], counts[0]
    for k in range(1, N_GROUPS):
        off_g = jnp.where(g == k, offs[k], off_g)
        cnt_g = jnp.where(g == k, counts[k], cnt_g)
    lane = lax.broadcasted_iota(jnp.int32, (ch, LANES), 1)

    def chunk(c, carry):
        rows = pl.ds(pl.multiple_of(off_g + c * ch, SEG_ALIGN), ch)
        x = xs_ref[rows, :]
        cw = cw_ref[rows, :]
        y = jnp.zeros((ch, d), F32)
        for e in range(N_PER_GROUP):
            c_e = jnp.sum(jnp.where(lane == EXPERT_LANE0 + g * N_PER_GROUP + e, cw, 0.0), axis=1, keepdims=True)
            ge = g * N_PER_GROUP + e
            gate = jnp.dot(x, wg_ref[ge], preferred_element_type=F32)
            up = jnp.dot(x, wu_ref[ge], preferred_element_type=F32)
            hid = gate * jax.nn.sigmoid(gate) * up * c_e
            y = y + jnp.dot(hid.astype(BF16), wd_ref[ge], preferred_element_type=F32)
        ys_ref[rows, :] = y.astype(BF16)
        return carry

    lax.fori_loop(0, (cnt_g + ch - 1) // ch, chunk, 0)

    @pl.when(g == pl.num_programs(1) - 1)
    def _():
        pos_t = jnp.concatenate([pos_ref[...]] * (n_map // LANES), axis=1)
        cols = lax.broadcasted_iota(jnp.int32, (t, n_map), 1)
        onehot = jnp.where(cols.astype(F32) == pos_t, 1.0, 0.0).astype(BF16)
        out_ref[...] = jnp.dot(onehot, ys_ref[0:n_map, :], preferred_element_type=F32).astype(out_ref.dtype)


def _moe(n, comb, counts, w_g, w_u, w_d, tm):
    n_tok, d = n.shape
    ff = w_g.shape[2]
    ch = min(MOE_CHUNK, tm)
    n_map = -(-(tm + N_GROUPS * SEG_ALIGN) // LANES) * LANES
    r_rows = n_map + ch
    rb = next(r for r in MOE_ROW_BLOCKS if n_map % r == 0)
    row = lambda i, g, cnt: (i, 0)
    resident = dict(index_map=lambda i, g, cnt: (0, 0, 0), pipeline_mode=pl.Buffered(1))
    grid_spec = pltpu.PrefetchScalarGridSpec(
        num_scalar_prefetch=1, grid=(n_tok // tm, N_GROUPS),
        in_specs=[pl.BlockSpec((tm, d), row), pl.BlockSpec((tm, LANES), row),
                  pl.BlockSpec(w_g.shape, **resident), pl.BlockSpec(w_u.shape, **resident),
                  pl.BlockSpec(w_d.shape, **resident)],
        out_specs=pl.BlockSpec((tm, d), row),
        scratch_shapes=[pltpu.VMEM((tm, tm), BF16), pltpu.VMEM((r_rows, d), BF16), pltpu.VMEM((r_rows, d), BF16),
                        pltpu.VMEM((r_rows, LANES), F32), pltpu.VMEM((tm, LANES), F32)])
    return pl.pallas_call(
        functools.partial(_moe_kernel, ch=ch, rb=rb, n_map=n_map), grid_spec=grid_spec,
        out_shape=jax.ShapeDtypeStruct((n_tok, d), BF16),
        compiler_params=_params(("arbitrary", "arbitrary")), name="experts",
    )(counts, n, comb, w_g, w_u, w_d)


def _ple_kernel(h_ref, moe_ref, p_ref, g_ref, wpg_ref, wpp_ref, gf_ref, out_ref, *, final):
    h = h_ref[...] + moe_ref[...].astype(F32)
    n = _rms(h, g_ref[...])
    gate = jax.nn.sigmoid(jnp.dot(n.astype(BF16), wpg_ref[...], preferred_element_type=F32))
    emb = jnp.dot(p_ref[...].astype(BF16), wpp_ref[...], preferred_element_type=F32)
    h = h + gate * emb
    out_ref[...] = _rms(h, gf_ref[...]) if final else h


def _ple(h, moe, p, g_ple, w_pg, w_pp, g_final, final, tm):
    n_tok, d = h.shape
    pd = p.shape[1]
    row = lambda i: (i, 0)
    fixed = lambda i: (0, 0)
    return pl.pallas_call(
        functools.partial(_ple_kernel, final=final), grid=(n_tok // tm,),
        in_specs=[pl.BlockSpec((tm, d), row), pl.BlockSpec((tm, d), row), pl.BlockSpec((tm, pd), row),
                  pl.BlockSpec((1, d), fixed), pl.BlockSpec((d, d), fixed), pl.BlockSpec((pd, d), fixed),
                  pl.BlockSpec((1, d), fixed)],
        out_specs=pl.BlockSpec((tm, d), row), out_shape=jax.ShapeDtypeStruct((n_tok, d), F32),
        compiler_params=_params(("parallel",)), name="ple",
    )(h, moe, p, g_ple, w_pg, w_pp, g_final)


def _token_tile(n_tok, want):
    tm = min(want, n_tok)
    assert n_tok % tm == 0, (n_tok, tm)
    return tm


def kernel(x_prompt, x_sample, cache_k, cache_v, state_pool, page_table, p_prompt, p_sample, g_mix_norm, w_in, lambda_q1, lambda_k1, lambda_q2, lambda_k2, g_subln, w_pool, pool_scale, w_out, g_ffn_norm, w_router_group, w_router_expert, w_gate_e, w_up_e, w_down_e, g_ple_norm, w_ple_gate, w_ple_proj, g_final):
    b, s, d = x_prompt.shape
    db, t_new, _ = x_sample.shape
    assert t_new == 1, "the decode kernel handles one new token per sequence"
    depth = w_in.shape[0]
    page = cache_k.shape[2]
    nh, hd = cache_k.shape[3], cache_k.shape[4]
    past_len = page_table.shape[1] * page
    pw = pool_scale.shape[1]
    aw = nh * hd
    ff = w_gate_e.shape[-1]

    tm_p = _token_tile(s, 512)
    tm_i = _token_tile(s, 1024)
    tm_s = _token_tile(db, 128)
    tq = _token_tile(s, 512)
    tm_e = _token_tile(b * s, 1024)
    n_pg = 16
    assert page_table.shape[1] % n_pg == 0

    tab_p = _rope_tables(jnp.arange(s))
    tab_s = _rope_tables(jnp.full((tm_s,), past_len))

    h_p = x_prompt.reshape(b * s, d)
    h_s = x_sample.reshape(db, d)
    outs = [[] for _ in range(6)]
    for l in range(depth):
        lam_init = 0.8 - 0.6 * math.exp(-0.3 * l)
        lam_vecs = jnp.stack([lambda_q1[l], lambda_k1[l], lambda_q2[l], lambda_k2[l]])
        g_sub = g_subln[l][None]
        w_in_l = w_in[l].astype(BF16)
        w_pool_l = w_pool[l].astype(BF16)
        scale_l = pool_scale[l][None]
        w_out_l = w_out[l].astype(BF16)
        w_r = jnp.concatenate([w_router_group[l], w_router_expert[l].reshape(d, N_EXPERTS)], axis=1)
        w_r = jnp.pad(w_r, ((0, 0), (0, LANES - w_r.shape[1])))
        w_r_hi = w_r.astype(BF16)
        w_r = jnp.concatenate([w_r_hi, (w_r - w_r_hi.astype(F32)).astype(BF16)], axis=1)
        w_g = w_gate_e[l].reshape(N_EXPERTS, d, ff).astype(BF16)
        w_u = w_up_e[l].reshape(N_EXPERTS, d, ff).astype(BF16)
        w_d = w_down_e[l].reshape(N_EXPERTS, ff, d).astype(BF16)
        w_pg = w_ple_gate[l].astype(BF16)
        w_pp = w_ple_proj[l].astype(BF16)
        final = l == depth - 1

        def channel(h, o, zp, p, tm_mix, tm_exp, tm_ple):
            h1, n2, comb, cnt = _mixout(h, o, zp, w_out_l, g_ffn_norm[l][None], w_r, tm_mix)
            cnt = cnt[:, 0, :N_GROUPS].reshape(-1, tm_exp // tm_mix, N_GROUPS).sum(axis=1).reshape(-1)
            moe = _moe(n2, comb, cnt, w_g, w_u, w_d, tm_exp)
            return _ple(h1, moe, p, g_ple_norm[l][None], w_pg, w_pp, g_final[None], final, tm_ple)

        q, k, v, kb, vb, zp, tails = _inproj(h_p, g_mix_norm[l][None], w_in_l, tab_p, aw, tm_i, s // tm_i,
                                             pool_args=(w_pool_l, scale_l))
        o = _attn_prompt(q.reshape(b, s, aw), kb.reshape(b, s, aw), vb.reshape(b, s, aw),
                         lam_vecs, g_sub, lam_init, tq)
        h_p = channel(h_p, o.reshape(b * s, aw), zp, p_prompt[l].reshape(b * s, -1), tm_p, tm_e, tm_p)
        outs[0].append(k.reshape(b, s, nh, hd))
        outs[1].append(v.reshape(b, s, nh, hd))
        outs[2].append(tails.reshape(b, s // tm_i, HALO, pw)[:, -1, HALO - POOL_CTX:])

        q, k, v, u = _inproj(h_s, g_mix_norm[l][None], w_in_l, tab_s, aw, tm_s, 1)
        o = _attn_decode(q, k, v, cache_k, cache_v, page_table, l, lam_vecs, g_sub, lam_init, n_pg)
        zp, pool_new = _pool_sample(state_pool[l], u, w_pool_l, scale_l, past_len)
        h_s = channel(h_s, o, zp, p_sample[l].reshape(db, -1), tm_s, tm_s, tm_s)
        outs[3].append(k.reshape(db, 1, nh, hd))
        outs[4].append(v.reshape(db, 1, nh, hd))
        outs[5].append(pool_new)

    y_prompt = h_p.reshape(b, s, d)
    y_sample = h_s.reshape(db, 1, d)
    return (y_prompt, y_sample, *[jnp.stack(o, axis=0) for o in outs])
```

```python
import functools
import math

import jax
import jax.numpy as jnp
from jax import lax
from jax.experimental import pallas as pl
from jax.experimental.pallas import tpu as pltpu

F32 = jnp.float32
BF16 = jnp.bfloat16

LANES = 128
SUBLANES = 8
DIFF_HEAD_DIM = 64
ROT_DIM = DIFF_HEAD_DIM // 4
ROT_HALF = ROT_DIM // 2
ROPE_THETA = 500000.0
POOL_WINDOWS = (2, 4, 8, 16)
POOL_CTX = max(POOL_WINDOWS) - 1
HALO = 16
RMS_EPS = 1e-6
N_GROUPS = 4
N_PER_GROUP = 4
N_EXPERTS = N_GROUPS * N_PER_GROUP
EXPERT_LANE0 = N_GROUPS
ROUTE_ROWS = 24
NEG_INF = float("-inf")
ATTN_HEADS_PER_STEP = 2
MOE_CHUNK = 288
MOE_ROW_BLOCKS = (512, 384, 256, 128)
VMEM_LIMIT = 56 * 1024 * 1024


def _rms(x, g):
    return x * lax.rsqrt(jnp.mean(x * x, axis=-1, keepdims=True) + RMS_EPS) * g


def _params(sem):
    return pltpu.CompilerParams(dimension_semantics=sem, vmem_limit_bytes=VMEM_LIMIT)


def _rope_tables(pos):
    inv = ROPE_THETA ** (-jnp.arange(0, ROT_DIM, 2, dtype=F32) / ROT_DIM)
    ang = pos.astype(F32)[:, None] * inv[None, :]
    cos, sin = jnp.cos(ang), jnp.sin(ang)
    lane = jnp.arange(LANES)
    r = lane % DIFF_HEAD_DIM
    i = r % ROT_HALF
    c = jnp.where(r < ROT_DIM, cos[:, i], 1.0)
    sa = jnp.where(r < ROT_HALF, -sin[:, i], 0.0)
    sb = jnp.where((r >= ROT_HALF) & (r < ROT_DIM), sin[:, i], 0.0)
    return c, sa, sb


def _pool_mix(ue_ref, u, pos0, w_pool_ref, scale_ref, zp_ref):
    tm = u.shape[0]
    pos = pos0 + lax.broadcasted_iota(jnp.int32, (tm, 1), 0)
    for g, w in enumerate(POOL_WINDOWS):
        cols = slice(g * LANES, (g + 1) * LANES)
        acc = u[:, cols]
        for k in range(1, w):
            acc = acc + ue_ref[HALO - k:HALO - k + tm, cols]
        cnt = jnp.minimum(w, pos + 1).astype(F32)
        z = acc / cnt - u[:, cols]
        y = jnp.dot(z.astype(BF16), w_pool_ref[g], preferred_element_type=F32)
        zp_ref[:, cols] = (y * scale_ref[:, cols]).astype(BF16)


def _inproj_kernel(x_ref, g_ref, w_ref, c_ref, sa_ref, sb_ref, *rest, pool, tiles_per_seq):
    if pool:
        w_pool_ref, scale_ref, q_ref, k_ref, v_ref, kb_ref, vb_ref, zp_ref, tail_ref, ue_ref = rest
    else:
        q_ref, k_ref, v_ref, u_ref = rest
    tm = x_ref.shape[0]
    aw = q_ref.shape[1]
    nh = aw // LANES
    n = _rms(x_ref[...], g_ref[...])
    z = jnp.dot(n.astype(BF16), w_ref[...], preferred_element_type=F32)
    c, sa, sb = c_ref[...], sa_ref[...], sb_ref[...]

    def rope(blk):
        return blk * c + pltpu.roll(blk, LANES - ROT_HALF, 1) * sa + pltpu.roll(blk, ROT_HALF, 1) * sb

    q_scale = DIFF_HEAD_DIM ** -0.5
    for j in range(nh):
        cols = slice(j * LANES, (j + 1) * LANES)
        q_ref[:, cols] = (rope(z[:, j * LANES:(j + 1) * LANES]) * q_scale).astype(BF16)
        k = rope(z[:, aw + j * LANES:aw + (j + 1) * LANES])
        v = z[:, 2 * aw + j * LANES:2 * aw + (j + 1) * LANES]
        k_ref[pl.ds(j, tm, stride=nh), :] = k
        v_ref[pl.ds(j, tm, stride=nh), :] = v
        if pool:
            kb_ref[:, cols] = k.astype(BF16)
            vb_ref[:, cols] = v.astype(BF16)
    u = z[:, 3 * aw:]
    if not pool:
        u_ref[...] = u
        return
    t = pl.program_id(0) % tiles_per_seq

    @pl.when(t == 0)
    def _():
        ue_ref[0:HALO, :] = jnp.zeros((HALO, u.shape[1]), F32)

    @pl.when(t != 0)
    def _():
        ue_ref[0:HALO, :] = ue_ref[tm:tm + HALO, :]

    ue_ref[HALO:HALO + tm, :] = u
    tail_ref[0] = u[tm - HALO:, :]
    _pool_mix(ue_ref, u, t * tm, w_pool_ref, scale_ref, zp_ref)


def _inproj(x, g, w_in, tables, aw, tm, tiles_per_seq, pool_args=None):
    n_tok, d = x.shape
    in_w = w_in.shape[1]
    pw = in_w - 3 * aw
    nh = aw // LANES
    pool = pool_args is not None
    grid = (n_tok // tm,)
    row = lambda i: (i, 0)
    fixed = lambda i: (0, 0)
    tab = lambda i: (i % tiles_per_seq, 0)
    in_specs = [pl.BlockSpec((tm, d), row), pl.BlockSpec((1, d), fixed), pl.BlockSpec((d, in_w), fixed),
                pl.BlockSpec((tm, LANES), tab), pl.BlockSpec((tm, LANES), tab), pl.BlockSpec((tm, LANES), tab)]
    args = [x, g, w_in, *tables]
    out_shape = [jax.ShapeDtypeStruct((n_tok, aw), BF16), jax.ShapeDtypeStruct((n_tok * nh, LANES), F32),
                 jax.ShapeDtypeStruct((n_tok * nh, LANES), F32)]
    out_specs = [pl.BlockSpec((tm, aw), row), pl.BlockSpec((tm * nh, LANES), row), pl.BlockSpec((tm * nh, LANES), row)]
    scratch = []
    if pool:
        w_pool, scale = pool_args
        in_specs += [pl.BlockSpec(w_pool.shape, lambda i: (0, 0, 0)), pl.BlockSpec((1, pw), fixed)]
        args += [w_pool, scale]
        out_shape += [jax.ShapeDtypeStruct((n_tok, aw), BF16), jax.ShapeDtypeStruct((n_tok, aw), BF16),
                      jax.ShapeDtypeStruct((n_tok, pw), BF16), jax.ShapeDtypeStruct((grid[0], HALO, pw), F32)]
        out_specs += [pl.BlockSpec((tm, aw), row), pl.BlockSpec((tm, aw), row),
                      pl.BlockSpec((tm, pw), row), pl.BlockSpec((1, HALO, pw), lambda i: (i, 0, 0))]
        scratch = [pltpu.VMEM((tm + HALO, pw), F32)]
    else:
        out_shape += [jax.ShapeDtypeStruct((n_tok, pw), F32)]
        out_specs += [pl.BlockSpec((tm, pw), row)]
    return pl.pallas_call(
        functools.partial(_inproj_kernel, pool=pool, tiles_per_seq=tiles_per_seq),
        grid=grid, in_specs=in_specs, out_specs=out_specs, out_shape=out_shape, scratch_shapes=scratch,
        compiler_params=_params(("arbitrary",)), name="inproj_pool" if pool else "inproj",
    )(*args)


def _diff_lambda(lam_ref, lam_init):
    lv = lam_ref[...]
    a = jnp.exp(jnp.sum(lv[0:1] * lv[1:2], axis=-1, keepdims=True))
    b = jnp.exp(jnp.sum(lv[2:3] * lv[3:4], axis=-1, keepdims=True))
    return a - b + lam_init


def _sub_norm(w, g, lam_init):
    return _rms(w, g) * (1.0 - lam_init)


def _stack_maps(q):
    lane = lax.broadcasted_iota(jnp.int32, q.shape, 1)
    first = jnp.where(lane < DIFF_HEAD_DIM, 1.0, 0.0).astype(q.dtype)
    return jnp.concatenate([q * first, q * (1.0 - first)], axis=0)


def _attn_prompt_kernel(lam_ref, g_ref, q_ref, k_ref, v_ref, o_ref, kb_ref, vb_ref, m_ref, acc_ref, s_ref,
                        *, lam_init):
    tq = o_ref.shape[1]
    nhs = o_ref.shape[2] // LANES
    qi = pl.program_id(2)
    nq = pl.num_programs(2)

    def stacked_q(blk):
        rows = pl.ds(pl.multiple_of(blk * tq, tq), tq)
        return [_stack_maps(q_ref[0, rows, h * LANES:(h + 1) * LANES]) for h in range(nhs)]

    def scores_of(qs, h, j):
        start = pl.multiple_of(j * tq, tq)
        return lax.dot_general(qs[h], kb_ref[h, pl.ds(start, tq), :], (((1,), (1,)), ((), ())),
                               preferred_element_type=F32)

    qq = stacked_q(qi)
    scores = functools.partial(scores_of, qq)

    @pl.when(qi == 0)
    def _():
        for h in range(nhs):
            cols = slice(h * LANES, (h + 1) * LANES)
            kb_ref[h] = k_ref[0, :, cols]
            vb_ref[h, :, 0:LANES] = v_ref[0, :, cols]
            vb_ref[h, :, LANES:] = jnp.ones((vb_ref.shape[1], LANES), BF16)
        for h in range(nhs):
            s_ref[h] = scores(h, 0)

    m_ref[...] = jnp.full(m_ref.shape, NEG_INF, F32)
    acc_ref[...] = jnp.zeros(acc_ref.shape, F32)

    def update(h, j, masked):
        start = pl.multiple_of(j * tq, tq)

        def load_scores():
            s = s_ref[h]
            if masked:
                r = lax.broadcasted_iota(jnp.int32, s.shape, 0)
                r = jnp.where(r >= tq, r - tq, r)
                c = lax.broadcasted_iota(jnp.int32, s.shape, 1)
                s = jnp.where(c <= r, s, NEG_INF)
            return s

        m_prev = m_ref[h]
        m_new = jnp.maximum(m_prev, jnp.max(load_scores(), axis=1, keepdims=True))
        alpha = jnp.exp(m_prev - m_new)
        p = jnp.exp(load_scores() - jnp.concatenate([m_new] * (tq // LANES), axis=1))
        pv = jnp.dot(p.astype(BF16), vb_ref[h, pl.ds(start, tq), :], preferred_element_type=F32)
        acc_ref[h] = jnp.concatenate([alpha, alpha], axis=1) * acc_ref[h] + pv
        m_ref[h] = m_new

    def body(j, carry):
        for h in range(nhs):
            update(h, j, masked=False)
            s_ref[h] = scores(h, j + 1)
        return carry

    lax.fori_loop(0, qi, body, 0)
    lam = _diff_lambda(lam_ref, lam_init)
    q_next = stacked_q(jnp.minimum(qi + 1, nq - 1))
    for h in range(nhs):
        update(h, qi, masked=True)
        s_ref[h] = scores_of(q_next, h, 0)
        acc = acc_ref[h]
        o = acc[:, 0:LANES] / acc[:, LANES:]
        w = o[:tq] - lam * o[tq:]
        o_ref[0, :, h * LANES:(h + 1) * LANES] = _sub_norm(w, g_ref[...], lam_init).astype(BF16)


def _attn_prompt(q, k, v, lam_vecs, g_sub, lam_init, tq):
    b, s, aw = q.shape
    nhs = ATTN_HEADS_PER_STEP
    hw = nhs * LANES
    grid = (b, aw // hw, s // tq)
    fixed = lambda bi, h, qi: (0, 0)
    whole_seq = lambda bi, h, qi: (bi, 0, h)
    return pl.pallas_call(
        functools.partial(_attn_prompt_kernel, lam_init=lam_init),
        grid=grid,
        in_specs=[pl.BlockSpec(lam_vecs.shape, fixed), pl.BlockSpec(g_sub.shape, fixed),
                  pl.BlockSpec((1, s, hw), whole_seq), pl.BlockSpec((1, s, hw), whole_seq),
                  pl.BlockSpec((1, s, hw), whole_seq)],
        out_specs=pl.BlockSpec((1, tq, hw), lambda bi, h, qi: (bi, qi, h)),
        out_shape=jax.ShapeDtypeStruct((b, s, aw), BF16),
        scratch_shapes=[pltpu.VMEM((nhs, s, LANES), BF16), pltpu.VMEM((nhs, s, 2 * LANES), BF16),
                        pltpu.VMEM((nhs, 2 * tq, LANES), F32), pltpu.VMEM((nhs, 2 * tq, 2 * LANES), F32),
                        pltpu.VMEM((nhs, 2 * tq, tq), F32)],
        compiler_params=_params(("parallel", "parallel", "arbitrary")), name="attn_prompt",
    )(lam_vecs, g_sub, q, k, v)


def _attn_decode_kernel(pt_ref, lam_ref, g_ref, q_ref, kn_ref, vn_ref, *rest, lam_init, n_pg):
    k_refs, v_refs = rest[:n_pg], rest[n_pg:2 * n_pg]
    o_ref, m_ref, l_ref, acc_ref = rest[2 * n_pg:]
    ci = pl.program_id(1)
    nh = q_ref.shape[1]
    nmap = 2 * nh

    def per_map(x):
        return jnp.concatenate([x[h:h + 1] for h in range(nh) for _ in range(2)], axis=0)

    row = lax.broadcasted_iota(jnp.int32, (nmap, LANES), 0)
    lane = lax.broadcasted_iota(jnp.int32, (nmap, LANES), 1)
    qm_f32 = jnp.where(lane // DIFF_HEAD_DIM == row % 2, per_map(q_ref[0].astype(F32)), 0.0)
    qm = qm_f32.astype(BF16)

    @pl.when(ci == 0)
    def _():
        m_ref[...] = jnp.sum(qm_f32 * per_map(kn_ref[0]), axis=1, keepdims=True)
        l_ref[...] = jnp.ones(l_ref.shape, F32)
        acc_ref[...] = per_map(vn_ref[0])

    s = jnp.concatenate(
        [lax.dot_general(qm, kr[0].astype(BF16), (((1,), (1,)), ((), ())), preferred_element_type=F32)
         for kr in k_refs], axis=1)
    srow = lax.broadcasted_iota(jnp.int32, s.shape, 0)
    scol = lax.broadcasted_iota(jnp.int32, s.shape, 1)
    s = jnp.where(scol % nh == srow // 2, s, NEG_INF)
    m_prev = m_ref[...]
    m_new = jnp.maximum(m_prev, jnp.max(s, axis=1, keepdims=True))
    alpha = jnp.exp(m_prev - m_new)
    p = jnp.exp(s - m_new)
    l_ref[...] = alpha * l_ref[...] + jnp.sum(p, axis=1, keepdims=True)
    pb = p.astype(BF16)
    rows = k_refs[0].shape[1]
    pv = jnp.dot(pb[:, 0:rows], v_refs[0][0].astype(BF16), preferred_element_type=F32)
    for i in range(1, n_pg):
        pv = pv + jnp.dot(pb[:, i * rows:(i + 1) * rows], v_refs[i][0].astype(BF16), preferred_element_type=F32)
    acc_ref[...] = alpha * acc_ref[...] + pv
    m_ref[...] = m_new

    @pl.when(ci == pl.num_programs(1) - 1)
    def _():
        o = acc_ref[...] / l_ref[...]
        lam = _diff_lambda(lam_ref, lam_init)
        for h in range(nh):
            w = o[2 * h:2 * h + 1] - lam * o[2 * h + 1:2 * h + 2]
            o_ref[0, :, h * LANES:(h + 1) * LANES] = _sub_norm(w, g_ref[...], lam_init).astype(BF16)


def _attn_decode(q, k_new, v_new, cache_k, cache_v, page_table, layer, lam_vecs, g_sub, lam_init, n_pg):
    db, aw = q.shape
    _, n_pool, page, nh, hd = cache_k.shape
    n_pages = page_table.shape[1]
    ck = cache_k.reshape(-1, page * nh, hd)
    cv = cache_v.reshape(-1, page * nh, hd)
    pt = page_table.reshape(-1) + layer * n_pool
    fixed = lambda b, c, pt_ref: (0, 0)
    per_seq = lambda b, c, pt_ref: (b, 0, 0)

    def page_spec(i):
        return pl.BlockSpec((1, page * nh, hd), lambda b, c, pt_ref: (pt_ref[b * n_pages + c * n_pg + i], 0, 0))

    grid_spec = pltpu.PrefetchScalarGridSpec(
        num_scalar_prefetch=1, grid=(db, n_pages // n_pg),
        in_specs=[pl.BlockSpec(lam_vecs.shape, fixed), pl.BlockSpec(g_sub.shape, fixed),
                  pl.BlockSpec((1, nh, hd), per_seq), pl.BlockSpec((1, nh, hd), per_seq),
                  pl.BlockSpec((1, nh, hd), per_seq)]
        + [page_spec(i) for i in range(n_pg)] * 2,
        out_specs=pl.BlockSpec((1, 1, aw), per_seq),
        scratch_shapes=[pltpu.VMEM((2 * nh, 1), F32), pltpu.VMEM((2 * nh, 1), F32), pltpu.VMEM((2 * nh, hd), F32)])
    out = pl.pallas_call(
        functools.partial(_attn_decode_kernel, lam_init=lam_init, n_pg=n_pg),
        grid_spec=grid_spec, out_shape=jax.ShapeDtypeStruct((db, 1, aw), BF16),
        compiler_params=_params(("parallel", "arbitrary")), name="attn_decode",
    )(pt, lam_vecs, g_sub, q.reshape(db, nh, hd), k_new.reshape(db, nh, hd), v_new.reshape(db, nh, hd),
      *([ck] * n_pg), *([cv] * n_pg))
    return out.reshape(db, aw)


def _pool_sample_kernel(st_ref, u_ref, w_pool_ref, scale_ref, zp_ref, new_ref, *, pos):
    pw = u_ref.shape[1]
    u = u_ref[...]
    for g, w in enumerate(POOL_WINDOWS):
        cols = slice(g * LANES, (g + 1) * LANES)
        acc = u[:, cols]
        for k in range(1, w):
            r = POOL_CTX - k
            acc = acc + st_ref[:, r * pw + g * LANES:r * pw + (g + 1) * LANES]
        z = acc / float(min(w, pos + 1)) - u[:, cols]
        y = jnp.dot(z.astype(BF16), w_pool_ref[g], preferred_element_type=F32)
        zp_ref[:, cols] = (y * scale_ref[:, cols]).astype(BF16)
    new_ref[:, 0:(POOL_CTX - 1) * pw] = st_ref[:, pw:POOL_CTX * pw]
    new_ref[:, (POOL_CTX - 1) * pw:] = u


def _pool_sample(state, u, w_pool, scale, pos):
    db, ctx, pw = state.shape
    st2 = state.reshape(db, ctx * pw)
    zp, new = pl.pallas_call(
        functools.partial(_pool_sample_kernel, pos=pos),
        out_shape=[jax.ShapeDtypeStruct((db, pw), BF16), jax.ShapeDtypeStruct((db, ctx * pw), F32)],
        compiler_params=pltpu.CompilerParams(vmem_limit_bytes=VMEM_LIMIT), name="pool_sample",
    )(st2, u, w_pool, scale)
    return zp, new.reshape(db, ctx, pw)


def _route(lg):
    tm = lg.shape[0]
    lt = lg.T[0:ROUTE_ROWS, :]
    row = lax.broadcasted_iota(jnp.int32, lt.shape, 0)
    big = jnp.int32(LANES)
    is_g = row < N_GROUPS
    mg = jnp.max(jnp.where(is_g, lt, NEG_INF), axis=0, keepdims=True)
    g_sel = jnp.min(jnp.where(is_g & (lt == mg), row, big), axis=0, keepdims=True)
    g_w = 1.0 / jnp.sum(jnp.where(is_g, jnp.exp(lt - mg), 0.0), axis=0, keepdims=True)
    lo = EXPERT_LANE0 + g_sel * N_PER_GROUP
    in_grp = (row >= lo) & (row < lo + N_PER_GROUP)
    v1 = jnp.max(jnp.where(in_grp, lt, NEG_INF), axis=0, keepdims=True)
    i1 = jnp.min(jnp.where(in_grp & (lt == v1), row, big), axis=0, keepdims=True)
    rest = in_grp & (row != i1)
    v2 = jnp.max(jnp.where(rest, lt, NEG_INF), axis=0, keepdims=True)
    i2 = jnp.min(jnp.where(rest & (lt == v2), row, big), axis=0, keepdims=True)
    e2 = jnp.exp(v2 - v1)
    w1 = g_w / (1.0 + e2)
    w2 = g_w * e2 / (1.0 + e2)
    comb = jnp.where(row == i1, w1, 0.0) + jnp.where(row == i2, w2, 0.0)
    comb = jnp.where(row == 0, g_sel.astype(F32), comb)
    comb = jnp.concatenate([comb, jnp.zeros((LANES - ROUTE_ROWS, tm), F32)], axis=0).T
    lane = lax.broadcasted_iota(jnp.int32, (1, LANES), 1)
    cnt = jnp.zeros((1, LANES), F32)
    for k in range(N_GROUPS):
        cnt = jnp.where(lane == k, jnp.sum(jnp.where(g_sel == k, 1.0, 0.0), axis=1, keepdims=True), cnt)
    return comb, cnt


def _mixout_kernel(x_ref, o_ref, zp_ref, wo_ref, g_ref, wr_ref, h_ref, n_ref, comb_ref, cnt_ref):
    aw = o_ref.shape[1]
    mix = jnp.dot(o_ref[...], wo_ref[0:aw, :], preferred_element_type=F32)
    mix = mix + jnp.dot(zp_ref[...], wo_ref[aw:, :], preferred_element_type=F32)
    h = x_ref[...] + mix
    h_ref[...] = h
    n = _rms(h, g_ref[...])
    nb = n.astype(BF16)
    n_ref[...] = nb
    tm = nb.shape[0]
    n_lo = (n - nb.astype(F32)).astype(BF16)
    r = jnp.dot(jnp.concatenate([nb, n_lo], axis=0), wr_ref[...], preferred_element_type=F32)
    lg = (r[:tm, :LANES] + r[:tm, LANES:]) + (r[tm:, :LANES] + r[tm:, LANES:])
    comb, cnt = _route(lg)
    comb_ref[...] = comb
    cnt_ref[0] = jnp.broadcast_to(cnt, cnt_ref.shape[1:]).astype(jnp.int32)


def _mixout(x, o, zp, w_out, g_ffn, w_router, tm):
    n_tok, d = x.shape
    aw, pw = o.shape[1], zp.shape[1]
    row = lambda i: (i, 0)
    fixed = lambda i: (0, 0)
    return pl.pallas_call(
        _mixout_kernel, grid=(n_tok // tm,),
        in_specs=[pl.BlockSpec((tm, d), row), pl.BlockSpec((tm, aw), row), pl.BlockSpec((tm, pw), row),
                  pl.BlockSpec(w_out.shape, fixed), pl.BlockSpec((1, d), fixed),
                  pl.BlockSpec(w_router.shape, fixed)],
        out_specs=[pl.BlockSpec((tm, d), row), pl.BlockSpec((tm, d), row), pl.BlockSpec((tm, LANES), row),
                   pl.BlockSpec((1, SUBLANES, LANES), lambda i: (i, 0, 0))],
        out_shape=[jax.ShapeDtypeStruct((n_tok, d), F32), jax.ShapeDtypeStruct((n_tok, d), BF16),
                   jax.ShapeDtypeStruct((n_tok, LANES), F32),
                   jax.ShapeDtypeStruct((n_tok // tm, SUBLANES, LANES), jnp.int32)],
        compiler_params=_params(("parallel",)), name="mixout",
    )(x, o, zp, w_out, g_ffn, w_router)


SEG_ALIGN = 16


def _moe_kernel(cnt_ref, n_ref, comb_ref, wg_ref, wu_ref, wd_ref, out_ref, tri_ref, xs_ref, ys_ref, cw_ref, pos_ref,
                *, ch, rb, n_map):
    i, g = pl.program_id(0), pl.program_id(1)
    t, d = n_ref.shape
    r_rows = xs_ref.shape[0]
    counts = [cnt_ref[i * N_GROUPS + k] for k in range(N_GROUPS)]
    offs = [jnp.int32(0)]
    for k in range(N_GROUPS - 1):
        offs.append(offs[-1] + (counts[k] + SEG_ALIGN - 1) // SEG_ALIGN * SEG_ALIGN)

    @pl.when((i == 0) & (g == 0))
    def _():
        r = lax.broadcasted_iota(jnp.int32, (t, t), 0)
        c = lax.broadcasted_iota(jnp.int32, (t, t), 1)
        tri_ref[...] = jnp.where(r < c, 1.0, 0.0).astype(BF16)
        xs_ref[n_map:, :] = jnp.zeros((r_rows - n_map, d), BF16)
        cw_ref[n_map:, :] = jnp.zeros((r_rows - n_map, LANES), F32)

    @pl.when(g == 0)
    def _():
        comb = comb_ref[...]
        g_lm = comb.T[0:1, :]
        sub = lax.broadcasted_iota(jnp.int32, (SUBLANES, t), 0)
        memb = jnp.where(g_lm == sub.astype(F32), 1.0, 0.0)
        rank = jnp.dot(memb.astype(BF16), tri_ref[...], preferred_element_type=F32)
        off_b = jnp.zeros((SUBLANES, t), F32)
        for k in range(1, N_GROUPS):
            off_b = jnp.where(sub == k, offs[k].astype(F32), off_b)
        pos = jnp.sum(memb * (rank + off_b), axis=0, keepdims=True)
        pos_ref[...] = jnp.broadcast_to(pos, (LANES, t)).T
        comb_hi = comb.astype(BF16)
        comb_hl = jnp.concatenate([comb_hi, (comb - comb_hi.astype(F32)).astype(BF16)], axis=1)
        x = n_ref[...]
        for r0 in range(0, n_map, rb):
            rows = r0 + lax.broadcasted_iota(jnp.int32, (rb, t), 0)
            onehot = jnp.where(rows.astype(F32) == pos, 1.0, 0.0).astype(BF16)
            xs_ref[r0:r0 + rb, :] = jnp.dot(onehot, x, preferred_element_type=F32).astype(BF16)
            cw = jnp.dot(onehot, comb_hl, preferred_element_type=F32)
            cw_ref[r0:r0 + rb, :] = cw[:, :LANES] + cw[:, LANES:]
        ys_ref[0:n_map, :] = jnp.zeros((n_map, d), BF16)

    off_g, cnt_g = offs[0], counts[0]
    for k in range(1, N_GROUPS):
        off_g = jnp.where(g == k, offs[k], off_g)
        cnt_g = jnp.where(g == k, counts[k], cnt_g)
    lane = lax.broadcasted_iota(jnp.int32, (ch, LANES), 1)

    def chunk(c, carry):
        rows = pl.ds(pl.multiple_of(off_g + c * ch, SEG_ALIGN), ch)
        x = xs_ref[rows, :]
        cw = cw_ref[rows, :]
        y = jnp.zeros((ch, d), F32)
        for e in range(N_PER_GROUP):
            c_e = jnp.sum(jnp.where(lane == EXPERT_LANE0 + g * N_PER_GROUP + e, cw, 0.0), axis=1, keepdims=True)
            ge = g * N_PER_GROUP + e
            gate = jnp.dot(x, wg_ref[ge], preferred_element_type=F32)
            up = jnp.dot(x, wu_ref[ge], preferred_element_type=F32)
            hid = gate * jax.nn.sigmoid(gate) * up * c_e
            y = y + jnp.dot(hid.astype(BF16), wd_ref[ge], preferred_element_type=F32)
        ys_ref[rows, :] = y.astype(BF16)
        return carry

    lax.fori_loop(0, (cnt_g + ch - 1) // ch, chunk, 0)

    @pl.when(g == pl.num_programs(1) - 1)
    def _():
        pos_t = jnp.concatenate([pos_ref[...]] * (n_map // LANES), axis=1)
        cols = lax.broadcasted_iota(jnp.int32, (t, n_map), 1)
        onehot = jnp.where(cols.astype(F32) == pos_t, 1.0, 0.0).astype(BF16)
        out_ref[...] = jnp.dot(onehot, ys_ref[0:n_map, :], preferred_element_type=F32).astype(out_ref.dtype)


def _moe(n, comb, counts, w_g, w_u, w_d, tm):
    n_tok, d = n.shape
    ff = w_g.shape[2]
    ch = min(MOE_CHUNK, tm)
    n_map = -(-(tm + N_GROUPS * SEG_ALIGN) // LANES) * LANES
    r_rows = n_map + ch
    rb = next(r for r in MOE_ROW_BLOCKS if n_map % r == 0)
    row = lambda i, g, cnt: (i, 0)
    resident = dict(index_map=lambda i, g, cnt: (0, 0, 0), pipeline_mode=pl.Buffered(1))
    grid_spec = pltpu.PrefetchScalarGridSpec(
        num_scalar_prefetch=1, grid=(n_tok // tm, N_GROUPS),
        in_specs=[pl.BlockSpec((tm, d), row), pl.BlockSpec((tm, LANES), row),
                  pl.BlockSpec(w_g.shape, **resident), pl.BlockSpec(w_u.shape, **resident),
                  pl.BlockSpec(w_d.shape, **resident)],
        out_specs=pl.BlockSpec((tm, d), row),
        scratch_shapes=[pltpu.VMEM((tm, tm), BF16), pltpu.VMEM((r_rows, d), BF16), pltpu.VMEM((r_rows, d), BF16),
                        pltpu.VMEM((r_rows, LANES), F32), pltpu.VMEM((tm, LANES), F32)])
    return pl.pallas_call(
        functools.partial(_moe_kernel, ch=ch, rb=rb, n_map=n_map), grid_spec=grid_spec,
        out_shape=jax.ShapeDtypeStruct((n_tok, d), BF16),
        compiler_params=_params(("arbitrary", "arbitrary")), name="experts",
    )(counts, n, comb, w_g, w_u, w_d)


def _ple_kernel(h_ref, moe_ref, p_ref, g_ref, wpg_ref, wpp_ref, gf_ref, out_ref, *, final):
    h = h_ref[...] + moe_ref[...].astype(F32)
    n = _rms(h, g_ref[...])
    gate = jax.nn.sigmoid(jnp.dot(n.astype(BF16), wpg_ref[...], preferred_element_type=F32))
    emb = jnp.dot(p_ref[...].astype(BF16), wpp_ref[...], preferred_element_type=F32)
    h = h + gate * emb
    out_ref[...] = _rms(h, gf_ref[...]) if final else h


def _ple(h, moe, p, g_ple, w_pg, w_pp, g_final, final, tm):
    n_tok, d = h.shape
    pd = p.shape[1]
    row = lambda i: (i, 0)
    fixed = lambda i: (0, 0)
    return pl.pallas_call(
        functools.partial(_ple_kernel, final=final), grid=(n_tok // tm,),
        in_specs=[pl.BlockSpec((tm, d), row), pl.BlockSpec((tm, d), row), pl.BlockSpec((tm, pd), row),
                  pl.BlockSpec((1, d), fixed), pl.BlockSpec((d, d), fixed), pl.BlockSpec((pd, d), fixed),
                  pl.BlockSpec((1, d), fixed)],
        out_specs=pl.BlockSpec((tm, d), row), out_shape=jax.ShapeDtypeStruct((n_tok, d), F32),
        compiler_params=_params(("parallel",)), name="ple",
    )(h, moe, p, g_ple, w_pg, w_pp, g_final)


def _token_tile(n_tok, want):
    tm = min(want, n_tok)
    assert n_tok % tm == 0, (n_tok, tm)
    return tm


def kernel(x_prompt, x_sample, cache_k, cache_v, state_pool, page_table, p_prompt, p_sample, g_mix_norm, w_in, lambda_q1, lambda_k1, lambda_q2, lambda_k2, g_subln, w_pool, pool_scale, w_out, g_ffn_norm, w_router_group, w_router_expert, w_gate_e, w_up_e, w_down_e, g_ple_norm, w_ple_gate, w_ple_proj, g_final):
    b, s, d = x_prompt.shape
    db, t_new, _ = x_sample.shape
    assert t_new == 1, "the decode kernel handles one new token per sequence"
    depth = w_in.shape[0]
    page = cache_k.shape[2]
    nh, hd = cache_k.shape[3], cache_k.shape[4]
    past_len = page_table.shape[1] * page
    pw = pool_scale.shape[1]
    aw = nh * hd
    ff = w_gate_e.shape[-1]

    tm_p = _token_tile(s, 512)
    tm_i = _token_tile(s, 1024)
    tm_s = _token_tile(db, 128)
    tq = _token_tile(s, 512)
    tm_e = _token_tile(b * s, 1024)
    n_pg = 16
    assert page_table.shape[1] % n_pg == 0

    tab_p = _rope_tables(jnp.arange(s))
    tab_s = _rope_tables(jnp.full((tm_s,), past_len))

    h_p = x_prompt.reshape(b * s, d)
    h_s = x_sample.reshape(db, d)
    outs = [[] for _ in range(6)]
    for l in range(depth):
        lam_init = 0.8 - 0.6 * math.exp(-0.3 * l)
        lam_vecs = jnp.stack([lambda_q1[l], lambda_k1[l], lambda_q2[l], lambda_k2[l]])
        g_sub = g_subln[l][None]
        w_in_l = w_in[l].astype(BF16)
        w_pool_l = w_pool[l].astype(BF16)
        scale_l = pool_scale[l][None]
        w_out_l = w_out[l].astype(BF16)
        w_r = jnp.concatenate([w_router_group[l], w_router_expert[l].reshape(d, N_EXPERTS)], axis=1)
        w_r = jnp.pad(w_r, ((0, 0), (0, LANES - w_r.shape[1])))
        w_r_hi = w_r.astype(BF16)
        w_r = jnp.concatenate([w_r_hi, (w_r - w_r_hi.astype(F32)).astype(BF16)], axis=1)
        w_g = w_gate_e[l].reshape(N_EXPERTS, d, ff).astype(BF16)
        w_u = w_up_e[l].reshape(N_EXPERTS, d, ff).astype(BF16)
        w_d = w_down_e[l].reshape(N_EXPERTS, ff, d).astype(BF16)
        w_pg = w_ple_gate[l].astype(BF16)
        w_pp = w_ple_proj[l].astype(BF16)
        final = l == depth - 1

        def channel(h, o, zp, p, tm_mix, tm_exp, tm_ple):
            h1, n2, comb, cnt = _mixout(h, o, zp, w_out_l, g_ffn_norm[l][None], w_r, tm_mix)
            cnt = cnt[:, 0, :N_GROUPS].reshape(-1, tm_exp // tm_mix, N_GROUPS).sum(axis=1).reshape(-1)
            moe = _moe(n2, comb, cnt, w_g, w_u, w_d, tm_exp)
            return _ple(h1, moe, p, g_ple_norm[l][None], w_pg, w_pp, g_final[None], final, tm_ple)

        q, k, v, kb, vb, zp, tails = _inproj(h_p, g_mix_norm[l][None], w_in_l, tab_p, aw, tm_i, s // tm_i,
                                             pool_args=(w_pool_l, scale_l))
        o = _attn_prompt(q.reshape(b, s, aw), kb.reshape(b, s, aw), vb.reshape(b, s, aw),
                         lam_vecs, g_sub, lam_init, tq)
        h_p = channel(h_p, o.reshape(b * s, aw), zp, p_prompt[l].reshape(b * s, -1), tm_p, tm_e, tm_p)
        outs[0].append(k.reshape(b, s, nh, hd))
        outs[1].append(v.reshape(b, s, nh, hd))
        outs[2].append(tails.reshape(b, s // tm_i, HALO, pw)[:, -1, HALO - POOL_CTX:])

        q, k, v, u = _inproj(h_s, g_mix_norm[l][None], w_in_l, tab_s, aw, tm_s, 1)
        o = _attn_decode(q, k, v, cache_k, cache_v, page_table, l, lam_vecs, g_sub, lam_init, n_pg)
        zp, pool_new = _pool_sample(state_pool[l], u, w_pool_l, scale_l, past_len)
        h_s = channel(h_s, o, zp, p_sample[l].reshape(db, -1), tm_s, tm_s, tm_s)
        outs[3].append(k.reshape(db, 1, nh, hd))
        outs[4].append(v.reshape(db, 1, nh, hd))
        outs[5].append(pool_new)

    y_prompt = h_p.reshape(b, s, d)
    y_sample = h_s.reshape(db, 1, d)
    return (y_prompt, y_sample, *[jnp.stack(o, axis=0) for o in outs])
```

```python
import functools
import math

import jax
import jax.numpy as jnp
from jax import lax
from jax.experimental import pallas as pl
from jax.experimental.pallas import tpu as pltpu

F32 = jnp.float32
BF16 = jnp.bfloat16

LANES = 128
SUBLANES = 8
DIFF_HEAD_DIM = 64
ROT_DIM = DIFF_HEAD_DIM // 4
ROT_HALF = ROT_DIM // 2
ROPE_THETA = 500000.0
POOL_WINDOWS = (2, 4, 8, 16)
POOL_CTX = max(POOL_WINDOWS) - 1
HALO = 16
RMS_EPS = 1e-6
N_GROUPS = 4
N_PER_GROUP = 4
N_EXPERTS = N_GROUPS * N_PER_GROUP
EXPERT_LANE0 = N_GROUPS
ROUTE_ROWS = 24
NEG_INF = float("-inf")
ATTN_HEADS_PER_STEP = 2
MOE_CHUNK = 288
MOE_ROW_BLOCKS = (512, 384, 256, 128)
VMEM_LIMIT = 56 * 1024 * 1024


def _rms(x, g):
    return x * lax.rsqrt(jnp.mean(x * x, axis=-1, keepdims=True) + RMS_EPS) * g


def _params(sem):
    return pltpu.CompilerParams(dimension_semantics=sem, vmem_limit_bytes=VMEM_LIMIT)


def _rope_tables(pos):
    inv = ROPE_THETA ** (-jnp.arange(0, ROT_DIM, 2, dtype=F32) / ROT_DIM)
    ang = pos.astype(F32)[:, None] * inv[None, :]
    cos, sin = jnp.cos(ang), jnp.sin(ang)
    lane = jnp.arange(LANES)
    r = lane % DIFF_HEAD_DIM
    i = r % ROT_HALF
    c = jnp.where(r < ROT_DIM, cos[:, i], 1.0)
    sa = jnp.where(r < ROT_HALF, -sin[:, i], 0.0)
    sb = jnp.where((r >= ROT_HALF) & (r < ROT_DIM), sin[:, i], 0.0)
    return c, sa, sb


def _pool_mix(ue_ref, u, pos0, w_pool_ref, scale_ref, zp_ref):
    tm = u.shape[0]
    pos = pos0 + lax.broadcasted_iota(jnp.int32, (tm, 1), 0)
    for g, w in enumerate(POOL_WINDOWS):
        cols = slice(g * LANES, (g + 1) * LANES)
        acc = u[:, cols]
        for k in range(1, w):
            acc = acc + ue_ref[HALO - k:HALO - k + tm, cols]
        cnt = jnp.minimum(w, pos + 1).astype(F32)
        z = acc / cnt - u[:, cols]
        y = jnp.dot(z.astype(BF16), w_pool_ref[g], preferred_element_type=F32)
        zp_ref[:, cols] = (y * scale_ref[:, cols]).astype(BF16)


def _inproj_kernel(x_ref, g_ref, w_ref, c_ref, sa_ref, sb_ref, *rest, pool, tiles_per_seq):
    if pool:
        w_pool_ref, scale_ref, q_ref, k_ref, v_ref, kb_ref, vb_ref, zp_ref, tail_ref, ue_ref = rest
    else:
        q_ref, k_ref, v_ref, u_ref = rest
    tm = x_ref.shape[0]
    aw = q_ref.shape[1]
    nh = aw // LANES
    n = _rms(x_ref[...], g_ref[...])
    z = jnp.dot(n.astype(BF16), w_ref[...], preferred_element_type=F32)
    c, sa, sb = c_ref[...], sa_ref[...], sb_ref[...]

    def rope(blk):
        return blk * c + pltpu.roll(blk, LANES - ROT_HALF, 1) * sa + pltpu.roll(blk, ROT_HALF, 1) * sb

    q_scale = DIFF_HEAD_DIM ** -0.5
    for j in range(nh):
        cols = slice(j * LANES, (j + 1) * LANES)
        q_ref[:, cols] = (rope(z[:, j * LANES:(j + 1) * LANES]) * q_scale).astype(BF16)
        k = rope(z[:, aw + j * LANES:aw + (j + 1) * LANES])
        v = z[:, 2 * aw + j * LANES:2 * aw + (j + 1) * LANES]
        k_ref[pl.ds(j, tm, stride=nh), :] = k
        v_ref[pl.ds(j, tm, stride=nh), :] = v
        if pool:
            kb_ref[:, cols] = k.astype(BF16)
            vb_ref[:, cols] = v.astype(BF16)
    u = z[:, 3 * aw:]
    if not pool:
        u_ref[...] = u
        return
    t = pl.program_id(0) % tiles_per_seq

    @pl.when(t == 0)
    def _():
        ue_ref[0:HALO, :] = jnp.zeros((HALO, u.shape[1]), F32)

    @pl.when(t != 0)
    def _():
        ue_ref[0:HALO, :] = ue_ref[tm:tm + HALO, :]

    ue_ref[HALO:HALO + tm, :] = u
    tail_ref[0] = u[tm - HALO:, :]
    _pool_mix(ue_ref, u, t * tm, w_pool_ref, scale_ref, zp_ref)


def _inproj(x, g, w_in, tables, aw, tm, tiles_per_seq, pool_args=None):
    n_tok, d = x.shape
    in_w = w_in.shape[1]
    pw = in_w - 3 * aw
    nh = aw // LANES
    pool = pool_args is not None
    grid = (n_tok // tm,)
    row = lambda i: (i, 0)
    fixed = lambda i: (0, 0)
    tab = lambda i: (i % tiles_per_seq, 0)
    in_specs = [pl.BlockSpec((tm, d), row), pl.BlockSpec((1, d), fixed), pl.BlockSpec((d, in_w), fixed),
                pl.BlockSpec((tm, LANES), tab), pl.BlockSpec((tm, LANES), tab), pl.BlockSpec((tm, LANES), tab)]
    args = [x, g, w_in, *tables]
    out_shape = [jax.ShapeDtypeStruct((n_tok, aw), BF16), jax.ShapeDtypeStruct((n_tok * nh, LANES), F32),
                 jax.ShapeDtypeStruct((n_tok * nh, LANES), F32)]
    out_specs = [pl.BlockSpec((tm, aw), row), pl.BlockSpec((tm * nh, LANES), row), pl.BlockSpec((tm * nh, LANES), row)]
    scratch = []
    if pool:
        w_pool, scale = pool_args
        in_specs += [pl.BlockSpec(w_pool.shape, lambda i: (0, 0, 0)), pl.BlockSpec((1, pw), fixed)]
        args += [w_pool, scale]
        out_shape += [jax.ShapeDtypeStruct((n_tok, aw), BF16), jax.ShapeDtypeStruct((n_tok, aw), BF16),
                      jax.ShapeDtypeStruct((n_tok, pw), BF16), jax.ShapeDtypeStruct((grid[0], HALO, pw), F32)]
        out_specs += [pl.BlockSpec((tm, aw), row), pl.BlockSpec((tm, aw), row),
                      pl.BlockSpec((tm, pw), row), pl.BlockSpec((1, HALO, pw), lambda i: (i, 0, 0))]
        scratch = [pltpu.VMEM((tm + HALO, pw), F32)]
    else:
        out_shape += [jax.ShapeDtypeStruct((n_tok, pw), F32)]
        out_specs += [pl.BlockSpec((tm, pw), row)]
    return pl.pallas_call(
        functools.partial(_inproj_kernel, pool=pool, tiles_per_seq=tiles_per_seq),
        grid=grid, in_specs=in_specs, out_specs=out_specs, out_shape=out_shape, scratch_shapes=scratch,
        compiler_params=_params(("arbitrary",)), name="inproj_pool" if pool else "inproj",
    )(*args)


def _diff_lambda(lam_ref, lam_init):
    lv = lam_ref[...]
    a = jnp.exp(jnp.sum(lv[0:1] * lv[1:2], axis=-1, keepdims=True))
    b = jnp.exp(jnp.sum(lv[2:3] * lv[3:4], axis=-1, keepdims=True))
    return a - b + lam_init


def _sub_norm(w, g, lam_init):
    return _rms(w, g) * (1.0 - lam_init)


def _stack_maps(q):
    lane = lax.broadcasted_iota(jnp.int32, q.shape, 1)
    first = jnp.where(lane < DIFF_HEAD_DIM, 1.0, 0.0).astype(q.dtype)
    return jnp.concatenate([q * first, q * (1.0 - first)], axis=0)


def _attn_prompt_kernel(lam_ref, g_ref, q_ref, k_ref, v_ref, o_ref, kb_ref, vb_ref, m_ref, acc_ref, s_ref,
                        *, lam_init):
    tq = o_ref.shape[1]
    nhs = o_ref.shape[2] // LANES
    qi = pl.program_id(2)
    nq = pl.num_programs(2)

    def stacked_q(blk):
        rows = pl.ds(pl.multiple_of(blk * tq, tq), tq)
        return [_stack_maps(q_ref[0, rows, h * LANES:(h + 1) * LANES]) for h in range(nhs)]

    def scores_of(qs, h, j):
        start = pl.multiple_of(j * tq, tq)
        return lax.dot_general(qs[h], kb_ref[h, pl.ds(start, tq), :], (((1,), (1,)), ((), ())),
                               preferred_element_type=F32)

    qq = stacked_q(qi)
    scores = functools.partial(scores_of, qq)

    @pl.when(qi == 0)
    def _():
        for h in range(nhs):
            cols = slice(h * LANES, (h + 1) * LANES)
            kb_ref[h] = k_ref[0, :, cols]
            vb_ref[h, :, 0:LANES] = v_ref[0, :, cols]
            vb_ref[h, :, LANES:] = jnp.ones((vb_ref.shape[1], LANES), BF16)
        for h in range(nhs):
            s_ref[h] = scores(h, 0)

    m_ref[...] = jnp.full(m_ref.shape, NEG_INF, F32)
    acc_ref[...] = jnp.zeros(acc_ref.shape, F32)

    def update(h, j, masked):
        start = pl.multiple_of(j * tq, tq)

        def load_scores():
            s = s_ref[h]
            if masked:
                r = lax.broadcasted_iota(jnp.int32, s.shape, 0)
                r = jnp.where(r >= tq, r - tq, r)
                c = lax.broadcasted_iota(jnp.int32, s.shape, 1)
                s = jnp.where(c <= r, s, NEG_INF)
            return s

        m_prev = m_ref[h]
        m_new = jnp.maximum(m_prev, jnp.max(load_scores(), axis=1, keepdims=True))
        alpha = jnp.exp(m_prev - m_new)
        p = jnp.exp(load_scores() - jnp.concatenate([m_new] * (tq // LANES), axis=1))
        pv = jnp.dot(p.astype(BF16), vb_ref[h, pl.ds(start, tq), :], preferred_element_type=F32)
        acc_ref[h] = jnp.concatenate([alpha, alpha], axis=1) * acc_ref[h] + pv
        m_ref[h] = m_new

    def body(j, carry):
        for h in range(nhs):
            update(h, j, masked=False)
            s_ref[h] = scores(h, j + 1)
        return carry

    lax.fori_loop(0, qi, body, 0)
    lam = _diff_lambda(lam_ref, lam_init)
    q_next = stacked_q(jnp.minimum(qi + 1, nq - 1))
    for h in range(nhs):
        update(h, qi, masked=True)
        s_ref[h] = scores_of(q_next, h, 0)
        acc = acc_ref[h]
        o = acc[:, 0:LANES] / acc[:, LANES:]
        w = o[:tq] - lam * o[tq:]
        o_ref[0, :, h * LANES:(h + 1) * LANES] = _sub_norm(w, g_ref[...], lam_init).astype(BF16)


def _attn_prompt(q, k, v, lam_vecs, g_sub, lam_init, tq):
    b, s, aw = q.shape
    nhs = ATTN_HEADS_PER_STEP
    hw = nhs * LANES
    grid = (b, aw // hw, s // tq)
    fixed = lambda bi, h, qi: (0, 0)
    whole_seq = lambda bi, h, qi: (bi, 0, h)
    return pl.pallas_call(
        functools.partial(_attn_prompt_kernel, lam_init=lam_init),
        grid=grid,
        in_specs=[pl.BlockSpec(lam_vecs.shape, fixed), pl.BlockSpec(g_sub.shape, fixed),
                  pl.BlockSpec((1, s, hw), whole_seq), pl.BlockSpec((1, s, hw), whole_seq),
                  pl.BlockSpec((1, s, hw), whole_seq)],
        out_specs=pl.BlockSpec((1, tq, hw), lambda bi, h, qi: (bi, qi, h)),
        out_shape=jax.ShapeDtypeStruct((b, s, aw), BF16),
        scratch_shapes=[pltpu.VMEM((nhs, s, LANES), BF16), pltpu.VMEM((nhs, s, 2 * LANES), BF16),
                        pltpu.VMEM((nhs, 2 * tq, LANES), F32), pltpu.VMEM((nhs, 2 * tq, 2 * LANES), F32),
                        pltpu.VMEM((nhs, 2 * tq, tq), F32)],
        compiler_params=_params(("parallel", "parallel", "arbitrary")), name="attn_prompt",
    )(lam_vecs, g_sub, q, k, v)


def _attn_decode_kernel(pt_ref, lam_ref, g_ref, q_ref, kn_ref, vn_ref, *rest, lam_init, n_pg):
    k_refs, v_refs = rest[:n_pg], rest[n_pg:2 * n_pg]
    o_ref, m_ref, l_ref, acc_ref = rest[2 * n_pg:]
    ci = pl.program_id(1)
    nh = q_ref.shape[1]
    nmap = 2 * nh

    def per_map(x):
        return jnp.concatenate([x[h:h + 1] for h in range(nh) for _ in range(2)], axis=0)

    row = lax.broadcasted_iota(jnp.int32, (nmap, LANES), 0)
    lane = lax.broadcasted_iota(jnp.int32, (nmap, LANES), 1)
    qm_f32 = jnp.where(lane // DIFF_HEAD_DIM == row % 2, per_map(q_ref[0].astype(F32)), 0.0)
    qm = qm_f32.astype(BF16)

    @pl.when(ci == 0)
    def _():
        m_ref[...] = jnp.sum(qm_f32 * per_map(kn_ref[0]), axis=1, keepdims=True)
        l_ref[...] = jnp.ones(l_ref.shape, F32)
        acc_ref[...] = per_map(vn_ref[0])

    s = jnp.concatenate(
        [lax.dot_general(qm, kr[0].astype(BF16), (((1,), (1,)), ((), ())), preferred_element_type=F32)
         for kr in k_refs], axis=1)
    srow = lax.broadcasted_iota(jnp.int32, s.shape, 0)
    scol = lax.broadcasted_iota(jnp.int32, s.shape, 1)
    s = jnp.where(scol % nh == srow // 2, s, NEG_INF)
    m_prev = m_ref[...]
    m_new = jnp.maximum(m_prev, jnp.max(s, axis=1, keepdims=True))
    alpha = jnp.exp(m_prev - m_new)
    p = jnp.exp(s - m_new)
    l_ref[...] = alpha * l_ref[...] + jnp.sum(p, axis=1, keepdims=True)
    pb = p.astype(BF16)
    rows = k_refs[0].shape[1]
    pv = jnp.dot(pb[:, 0:rows], v_refs[0][0].astype(BF16), preferred_element_type=F32)
    for i in range(1, n_pg):
        pv = pv + jnp.dot(pb[:, i * rows:(i + 1) * rows], v_refs[i][0].astype(BF16), preferred_element_type=F32)
    acc_ref[...] = alpha * acc_ref[...] + pv
    m_ref[...] = m_new

    @pl.when(ci == pl.num_programs(1) - 1)
    def _():
        o = acc_ref[...] / l_ref[...]
        lam = _diff_lambda(lam_ref, lam_init)
        for h in range(nh):
            w = o[2 * h:2 * h + 1] - lam * o[2 * h + 1:2 * h + 2]
            o_ref[0, :, h * LANES:(h + 1) * LANES] = _sub_norm(w, g_ref[...], lam_init).astype(BF16)


def _attn_decode(q, k_new, v_new, cache_k, cache_v, page_table, layer, lam_vecs, g_sub, lam_init, n_pg):
    db, aw = q.shape
    _, n_pool, page, nh, hd = cache_k.shape
    n_pages = page_table.shape[1]
    ck = cache_k.reshape(-1, page * nh, hd)
    cv = cache_v.reshape(-1, page * nh, hd)
    pt = page_table.reshape(-1) + layer * n_pool
    fixed = lambda b, c, pt_ref: (0, 0)
    per_seq = lambda b, c, pt_ref: (b, 0, 0)

    def page_spec(i):
        return pl.BlockSpec((1, page * nh, hd), lambda b, c, pt_ref: (pt_ref[b * n_pages + c * n_pg + i], 0, 0))

    grid_spec = pltpu.PrefetchScalarGridSpec(
        num_scalar_prefetch=1, grid=(db, n_pages // n_pg),
        in_specs=[pl.BlockSpec(lam_vecs.shape, fixed), pl.BlockSpec(g_sub.shape, fixed),
                  pl.BlockSpec((1, nh, hd), per_seq), pl.BlockSpec((1, nh, hd), per_seq),
                  pl.BlockSpec((1, nh, hd), per_seq)]
        + [page_spec(i) for i in range(n_pg)] * 2,
        out_specs=pl.BlockSpec((1, 1, aw), per_seq),
        scratch_shapes=[pltpu.VMEM((2 * nh, 1), F32), pltpu.VMEM((2 * nh, 1), F32), pltpu.VMEM((2 * nh, hd), F32)])
    out = pl.pallas_call(
        functools.partial(_attn_decode_kernel, lam_init=lam_init, n_pg=n_pg),
        grid_spec=grid_spec, out_shape=jax.ShapeDtypeStruct((db, 1, aw), BF16),
        compiler_params=_params(("parallel", "arbitrary")), name="attn_decode",
    )(pt, lam_vecs, g_sub, q.reshape(db, nh, hd), k_new.reshape(db, nh, hd), v_new.reshape(db, nh, hd),
      *([ck] * n_pg), *([cv] * n_pg))
    return out.reshape(db, aw)


def _pool_sample_kernel(st_ref, u_ref, w_pool_ref, scale_ref, zp_ref, new_ref, *, pos):
    pw = u_ref.shape[1]
    u = u_ref[...]
    for g, w in enumerate(POOL_WINDOWS):
        cols = slice(g * LANES, (g + 1) * LANES)
        acc = u[:, cols]
        for k in range(1, w):
            r = POOL_CTX - k
            acc = acc + st_ref[:, r * pw + g * LANES:r * pw + (g + 1) * LANES]
        z = acc / float(min(w, pos + 1)) - u[:, cols]
        y = jnp.dot(z.astype(BF16), w_pool_ref[g], preferred_element_type=F32)
        zp_ref[:, cols] = (y * scale_ref[:, cols]).astype(BF16)
    new_ref[:, 0:(POOL_CTX - 1) * pw] = st_ref[:, pw:POOL_CTX * pw]
    new_ref[:, (POOL_CTX - 1) * pw:] = u


def _pool_sample(state, u, w_pool, scale, pos):
    db, ctx, pw = state.shape
    st2 = state.reshape(db, ctx * pw)
    zp, new = pl.pallas_call(
        functools.partial(_pool_sample_kernel, pos=pos),
        out_shape=[jax.ShapeDtypeStruct((db, pw), BF16), jax.ShapeDtypeStruct((db, ctx * pw), F32)],
        compiler_params=pltpu.CompilerParams(vmem_limit_bytes=VMEM_LIMIT), name="pool_sample",
    )(st2, u, w_pool, scale)
    return zp, new.reshape(db, ctx, pw)


def _route(lg):
    tm = lg.shape[0]
    lt = lg.T[0:ROUTE_ROWS, :]
    row = lax.broadcasted_iota(jnp.int32, lt.shape, 0)
    big = jnp.int32(LANES)
    is_g = row < N_GROUPS
    mg = jnp.max(jnp.where(is_g, lt, NEG_INF), axis=0, keepdims=True)
    g_sel = jnp.min(jnp.where(is_g & (lt == mg), row, big), axis=0, keepdims=True)
    g_w = 1.0 / jnp.sum(jnp.where(is_g, jnp.exp(lt - mg), 0.0), axis=0, keepdims=True)
    lo = EXPERT_LANE0 + g_sel * N_PER_GROUP
    in_grp = (row >= lo) & (row < lo + N_PER_GROUP)
    v1 = jnp.max(jnp.where(in_grp, lt, NEG_INF), axis=0, keepdims=True)
    i1 = jnp.min(jnp.where(in_grp & (lt == v1), row, big), axis=0, keepdims=True)
    rest = in_grp & (row != i1)
    v2 = jnp.max(jnp.where(rest, lt, NEG_INF), axis=0, keepdims=True)
    i2 = jnp.min(jnp.where(rest & (lt == v2), row, big), axis=0, keepdims=True)
    e2 = jnp.exp(v2 - v1)
    w1 = g_w / (1.0 + e2)
    w2 = g_w * e2 / (1.0 + e2)
    comb = jnp.where(row == i1, w1, 0.0) + jnp.where(row == i2, w2, 0.0)
    comb = jnp.where(row == 0, g_sel.astype(F32), comb)
    comb = jnp.concatenate([comb, jnp.zeros((LANES - ROUTE_ROWS, tm), F32)], axis=0).T
    lane = lax.broadcasted_iota(jnp.int32, (1, LANES), 1)
    cnt = jnp.zeros((1, LANES), F32)
    for k in range(N_GROUPS):
        cnt = jnp.where(lane == k, jnp.sum(jnp.where(g_sel == k, 1.0, 0.0), axis=1, keepdims=True), cnt)
    return comb, cnt


def _mixout_kernel(x_ref, o_ref, zp_ref, wo_ref, g_ref, wr_ref, h_ref, n_ref, comb_ref, cnt_ref):
    aw = o_ref.shape[1]
    mix = jnp.dot(o_ref[...], wo_ref[0:aw, :], preferred_element_type=F32)
    mix = mix + jnp.dot(zp_ref[...], wo_ref[aw:, :], preferred_element_type=F32)
    h = x_ref[...] + mix
    h_ref[...] = h
    n = _rms(h, g_ref[...])
    nb = n.astype(BF16)
    n_ref[...] = nb
    tm = nb.shape[0]
    n_lo = (n - nb.astype(F32)).astype(BF16)
    r = jnp.dot(jnp.concatenate([nb, n_lo], axis=0), wr_ref[...], preferred_element_type=F32)
    lg = (r[:tm, :LANES] + r[:tm, LANES:]) + (r[tm:, :LANES] + r[tm:, LANES:])
    comb, cnt = _route(lg)
    comb_ref[...] = comb
    cnt_ref[0] = jnp.broadcast_to(cnt, cnt_ref.shape[1:]).astype(jnp.int32)


def _mixout(x, o, zp, w_out, g_ffn, w_router, tm):
    n_tok, d = x.shape
    aw, pw = o.shape[1], zp.shape[1]
    row = lambda i: (i, 0)
    fixed = lambda i: (0, 0)
    return pl.pallas_call(
        _mixout_kernel, grid=(n_tok // tm,),
        in_specs=[pl.BlockSpec((tm, d), row), pl.BlockSpec((tm, aw), row), pl.BlockSpec((tm, pw), row),
                  pl.BlockSpec(w_out.shape, fixed), pl.BlockSpec((1, d), fixed),
                  pl.BlockSpec(w_router.shape, fixed)],
        out_specs=[pl.BlockSpec((tm, d), row), pl.BlockSpec((tm, d), row), pl.BlockSpec((tm, LANES), row),
                   pl.BlockSpec((1, SUBLANES, LANES), lambda i: (i, 0, 0))],
        out_shape=[jax.ShapeDtypeStruct((n_tok, d), F32), jax.ShapeDtypeStruct((n_tok, d), BF16),
                   jax.ShapeDtypeStruct((n_tok, LANES), F32),
                   jax.ShapeDtypeStruct((n_tok // tm, SUBLANES, LANES), jnp.int32)],
        compiler_params=_params(("parallel",)), name="mixout",
    )(x, o, zp, w_out, g_ffn, w_router)


SEG_ALIGN = 16


def _moe_kernel(cnt_ref, n_ref, comb_ref, wg_ref, wu_ref, wd_ref, out_ref, tri_ref, xs_ref, ys_ref, cw_ref, pos_ref,
                *, ch, rb, n_map):
    i, g = pl.program_id(0), pl.program_id(1)
    t, d = n_ref.shape
    r_rows = xs_ref.shape[0]
    counts = [cnt_ref[i * N_GROUPS + k] for k in range(N_GROUPS)]
    offs = [jnp.int32(0)]
    for k in range(N_GROUPS - 1):
        offs.append(offs[-1] + (counts[k] + SEG_ALIGN - 1) // SEG_ALIGN * SEG_ALIGN)

    @pl.when((i == 0) & (g == 0))
    def _():
        r = lax.broadcasted_iota(jnp.int32, (t, t), 0)
        c = lax.broadcasted_iota(jnp.int32, (t, t), 1)
        tri_ref[...] = jnp.where(r < c, 1.0, 0.0).astype(BF16)
        xs_ref[n_map:, :] = jnp.zeros((r_rows - n_map, d), BF16)
        cw_ref[n_map:, :] = jnp.zeros((r_rows - n_map, LANES), F32)

    @pl.when(g == 0)
    def _():
        comb = comb_ref[...]
        g_lm = comb.T[0:1, :]
        sub = lax.broadcasted_iota(jnp.int32, (SUBLANES, t), 0)
        memb = jnp.where(g_lm == sub.astype(F32), 1.0, 0.0)
        rank = jnp.dot(memb.astype(BF16), tri_ref[...], preferred_element_type=F32)
        off_b = jnp.zeros((SUBLANES, t), F32)
        for k in range(1, N_GROUPS):
            off_b = jnp.where(sub == k, offs[k].astype(F32), off_b)
        pos = jnp.sum(memb * (rank + off_b), axis=0, keepdims=True)
        pos_ref[...] = jnp.broadcast_to(pos, (LANES, t)).T
        comb_hi = comb.astype(BF16)
        comb_hl = jnp.concatenate([comb_hi, (comb - comb_hi.astype(F32)).astype(BF16)], axis=1)
        x = n_ref[...]
        for r0 in range(0, n_map, rb):
            rows = r0 + lax.broadcasted_iota(jnp.int32, (rb, t), 0)
            onehot = jnp.where(rows.astype(F32) == pos, 1.0, 0.0).astype(BF16)
            xs_ref[r0:r0 + rb, :] = jnp.dot(onehot, x, preferred_element_type=F32).astype(BF16)
            cw = jnp.dot(onehot, comb_hl, preferred_element_type=F32)
            cw_ref[r0:r0 + rb, :] = cw[:, :LANES] + cw[:, LANES:]
        ys_ref[0:n_map, :] = jnp.zeros((n_map, d), BF16)

    off_g, cnt_g = offs[0], counts[0]
    for k in range(1, N_GROUPS):
        off_g = jnp.where(g == k, offs[k], off_g)
        cnt_g = jnp.where(g == k, counts[k], cnt_g)
    lane = lax.broadcasted_iota(jnp.int32, (ch, LANES), 1)

    def chunk(c, carry):
        rows = pl.ds(pl.multiple_of(off_g + c * ch, SEG_ALIGN), ch)
        x = xs_ref[rows, :]
        cw = cw_ref[rows, :]
        y = jnp.zeros((ch, d), F32)
        for e in range(N_PER_GROUP):
            c_e = jnp.sum(jnp.where(lane == EXPERT_LANE0 + g * N_PER_GROUP + e, cw, 0.0), axis=1, keepdims=True)
            ge = g * N_PER_GROUP + e
            gate = jnp.dot(x, wg_ref[ge], preferred_element_type=F32)
            up = jnp.dot(x, wu_ref[ge], preferred_element_type=F32)
            hid = gate * jax.nn.sigmoid(gate) * up * c_e
            y = y + jnp.dot(hid.astype(BF16), wd_ref[ge], preferred_element_type=F32)
        ys_ref[rows, :] = y.astype(BF16)
        return carry

    lax.fori_loop(0, (cnt_g + ch - 1) // ch, chunk, 0)

    @pl.when(g == pl.num_programs(1) - 1)
    def _():
        pos_t = jnp.concatenate([pos_ref[...]] * (n_map // LANES), axis=1)
        cols = lax.broadcasted_iota(jnp.int32, (t, n_map), 1)
        onehot = jnp.where(cols.astype(F32) == pos_t, 1.0, 0.0).astype(BF16)
        out_ref[...] = jnp.dot(onehot, ys_ref[0:n_map, :], preferred_element_type=F32).astype(out_ref.dtype)


def _moe(n, comb, counts, w_g, w_u, w_d, tm):
    n_tok, d = n.shape
    ff = w_g.shape[2]
    ch = min(MOE_CHUNK, tm)
    n_map = -(-(tm + N_GROUPS * SEG_ALIGN) // LANES) * LANES
    r_rows = n_map + ch
    rb = next(r for r in MOE_ROW_BLOCKS if n_map % r == 0)
    row = lambda i, g, cnt: (i, 0)
    resident = dict(index_map=lambda i, g, cnt: (0, 0, 0), pipeline_mode=pl.Buffered(1))
    grid_spec = pltpu.PrefetchScalarGridSpec(
        num_scalar_prefetch=1, grid=(n_tok // tm, N_GROUPS),
        in_specs=[pl.BlockSpec((tm, d), row), pl.BlockSpec((tm, LANES), row),
                  pl.BlockSpec(w_g.shape, **resident), pl.BlockSpec(w_u.shape, **resident),
                  pl.BlockSpec(w_d.shape, **resident)],
        out_specs=pl.BlockSpec((tm, d), row),
        scratch_shapes=[pltpu.VMEM((tm, tm), BF16), pltpu.VMEM((r_rows, d), BF16), pltpu.VMEM((r_rows, d), BF16),
                        pltpu.VMEM((r_rows, LANES), F32), pltpu.VMEM((tm, LANES), F32)])
    return pl.pallas_call(
        functools.partial(_moe_kernel, ch=ch, rb=rb, n_map=n_map), grid_spec=grid_spec,
        out_shape=jax.ShapeDtypeStruct((n_tok, d), BF16),
        compiler_params=_params(("arbitrary", "arbitrary")), name="experts",
    )(counts, n, comb, w_g, w_u, w_d)


def _ple_kernel(h_ref, moe_ref, p_ref, g_ref, wpg_ref, wpp_ref, gf_ref, out_ref, *, final):
    h = h_ref[...] + moe_ref[...].astype(F32)
    n = _rms(h, g_ref[...])
    gate = jax.nn.sigmoid(jnp.dot(n.astype(BF16), wpg_ref[...], preferred_element_type=F32))
    emb = jnp.dot(p_ref[...].astype(BF16), wpp_ref[...], preferred_element_type=F32)
    h = h + gate * emb
    out_ref[...] = _rms(h, gf_ref[...]) if final else h


def _ple(h, moe, p, g_ple, w_pg, w_pp, g_final, final, tm):
    n_tok, d = h.shape
    pd = p.shape[1]
    row = lambda i: (i, 0)
    fixed = lambda i: (0, 0)
    return pl.pallas_call(
        functools.partial(_ple_kernel, final=final), grid=(n_tok // tm,),
        in_specs=[pl.BlockSpec((tm, d), row), pl.BlockSpec((tm, d), row), pl.BlockSpec((tm, pd), row),
                  pl.BlockSpec((1, d), fixed), pl.BlockSpec((d, d), fixed), pl.BlockSpec((pd, d), fixed),
                  pl.BlockSpec((1, d), fixed)],
        out_specs=pl.BlockSpec((tm, d), row), out_shape=jax.ShapeDtypeStruct((n_tok, d), F32),
        compiler_params=_params(("parallel",)), name="ple",
    )(h, moe, p, g_ple, w_pg, w_pp, g_final)


def _token_tile(n_tok, want):
    tm = min(want, n_tok)
    assert n_tok % tm == 0, (n_tok, tm)
    return tm


def kernel(x_prompt, x_sample, cache_k, cache_v, state_pool, page_table, p_prompt, p_sample, g_mix_norm, w_in, lambda_q1, lambda_k1, lambda_q2, lambda_k2, g_subln, w_pool, pool_scale, w_out, g_ffn_norm, w_router_group, w_router_expert, w_gate_e, w_up_e, w_down_e, g_ple_norm, w_ple_gate, w_ple_proj, g_final):
    b, s, d = x_prompt.shape
    db, t_new, _ = x_sample.shape
    assert t_new == 1, "the decode kernel handles one new token per sequence"
    depth = w_in.shape[0]
    page = cache_k.shape[2]
    nh, hd = cache_k.shape[3], cache_k.shape[4]
    past_len = page_table.shape[1] * page
    pw = pool_scale.shape[1]
    aw = nh * hd
    ff = w_gate_e.shape[-1]

    tm_p = _token_tile(s, 512)
    tm_i = _token_tile(s, 1024)
    tm_s = _token_tile(db, 128)
    tq = _token_tile(s, 512)
    tm_e = _token_tile(b * s, 1024)
    n_pg = min(32, page_table.shape[1])
    assert page_table.shape[1] % n_pg == 0

    tab_p = _rope_tables(jnp.arange(s))
    tab_s = _rope_tables(jnp.full((tm_s,), past_len))

    h_p = x_prompt.reshape(b * s, d)
    h_s = x_sample.reshape(db, d)
    outs = [[] for _ in range(6)]
    for l in range(depth):
        lam_init = 0.8 - 0.6 * math.exp(-0.3 * l)
        lam_vecs = jnp.stack([lambda_q1[l], lambda_k1[l], lambda_q2[l], lambda_k2[l]])
        g_sub = g_subln[l][None]
        w_in_l = w_in[l].astype(BF16)
        w_pool_l = w_pool[l].astype(BF16)
        scale_l = pool_scale[l][None]
        w_out_l = w_out[l].astype(BF16)
        w_r = jnp.concatenate([w_router_group[l], w_router_expert[l].reshape(d, N_EXPERTS)], axis=1)
        w_r = jnp.pad(w_r, ((0, 0), (0, LANES - w_r.shape[1])))
        w_r_hi = w_r.astype(BF16)
        w_r = jnp.concatenate([w_r_hi, (w_r - w_r_hi.astype(F32)).astype(BF16)], axis=1)
        w_g = w_gate_e[l].reshape(N_EXPERTS, d, ff).astype(BF16)
        w_u = w_up_e[l].reshape(N_EXPERTS, d, ff).astype(BF16)
        w_d = w_down_e[l].reshape(N_EXPERTS, ff, d).astype(BF16)
        w_pg = w_ple_gate[l].astype(BF16)
        w_pp = w_ple_proj[l].astype(BF16)
        final = l == depth - 1

        def channel(h, o, zp, p, tm_mix, tm_exp, tm_ple):
            h1, n2, comb, cnt = _mixout(h, o, zp, w_out_l, g_ffn_norm[l][None], w_r, tm_mix)
            cnt = cnt[:, 0, :N_GROUPS].reshape(-1, tm_exp // tm_mix, N_GROUPS).sum(axis=1).reshape(-1)
            moe = _moe(n2, comb, cnt, w_g, w_u, w_d, tm_exp)
            return _ple(h1, moe, p, g_ple_norm[l][None], w_pg, w_pp, g_final[None], final, tm_ple)

        q, k, v, kb, vb, zp, tails = _inproj(h_p, g_mix_norm[l][None], w_in_l, tab_p, aw, tm_i, s // tm_i,
                                             pool_args=(w_pool_l, scale_l))
        o = _attn_prompt(q.reshape(b, s, aw), kb.reshape(b, s, aw), vb.reshape(b, s, aw),
                         lam_vecs, g_sub, lam_init, tq)
        h_p = channel(h_p, o.reshape(b * s, aw), zp, p_prompt[l].reshape(b * s, -1), tm_p, tm_e, tm_p)
        outs[0].append(k.reshape(b, s, nh, hd))
        outs[1].append(v.reshape(b, s, nh, hd))
        outs[2].append(tails.reshape(b, s // tm_i, HALO, pw)[:, -1, HALO - POOL_CTX:])

        q, k, v, u = _inproj(h_s, g_mix_norm[l][None], w_in_l, tab_s, aw, tm_s, 1)
        o = _attn_decode(q, k, v, cache_k, cache_v, page_table, l, lam_vecs, g_sub, lam_init, n_pg)
        zp, pool_new = _pool_sample(state_pool[l], u, w_pool_l, scale_l, past_len)
        h_s = channel(h_s, o, zp, p_sample[l].reshape(db, -1), tm_s, tm_s, tm_s)
        outs[3].append(k.reshape(db, 1, nh, hd))
        outs[4].append(v.reshape(db, 1, nh, hd))
        outs[5].append(pool_new)

    y_prompt = h_p.reshape(b, s, d)
    y_sample = h_s.reshape(db, 1, d)
    return (y_prompt, y_sample, *[jnp.stack(o, axis=0) for o in outs])
```
